```python
import jax, jax.numpy as jnp
from jax import lax
import numpy as np

D_MODEL = 1024
BATCH = 8
SEQ = 4096
DEPTH = 1

ATTN_HEADS = 8
ATTN_HEAD_DIM = 64
ATTN_WIDTH = ATTN_HEADS * ATTN_HEAD_DIM
DILATED_BRANCHES = ((128, 1), (512, 4), (2048, 16))
PAD_MULT = max(w for w, _ in DILATED_BRANCHES)
RET_HEADS = 4
RET_KEY_DIM = 64
RET_VALUE_DIM = 128
RET_QK_WIDTH = RET_HEADS * RET_KEY_DIM
RET_V_WIDTH = RET_HEADS * RET_VALUE_DIM
RET_CHUNK = 128
MIX_WIDTH = ATTN_WIDTH + RET_V_WIDTH
IN_SPLITS = (ATTN_WIDTH, ATTN_WIDTH, ATTN_WIDTH, ATTN_WIDTH,
             RET_QK_WIDTH, RET_QK_WIDTH, RET_V_WIDTH, RET_V_WIDTH)
IN_WIDTH = sum(IN_SPLITS)
ROPE_THETA = 10000.0
NORM_EPS = 1e-6
GN_EPS = 1e-5

kernel_name = "hymba_dilated_attn_retention_block"


def rms_norm(x, g):
    xf = x.astype(jnp.float32)
    y = xf * lax.rsqrt(jnp.mean(xf * xf, axis=-1, keepdims=True) + NORM_EPS)
    return (y * g.astype(jnp.float32)).astype(x.dtype)


def rope_half(x, pos):
    half = x.shape[-1] // 2
    inv = ROPE_THETA ** (-jnp.arange(half, dtype=jnp.float32) / half)
    ang = pos.astype(jnp.float32)[:, None] * inv[None, :]
    cos, sin = jnp.cos(ang).astype(x.dtype), jnp.sin(ang).astype(x.dtype)
    x1, x2 = x[..., :half], x[..., half:]
    return jnp.concatenate([x1 * cos - x2 * sin, x2 * cos + x1 * sin], axis=-1)


def retnet_rotate(x, pos):
    dk = x.shape[-1]
    half = dk // 2
    inv = 1.0 / (ROPE_THETA ** jnp.linspace(0.0, 1.0, half, dtype=jnp.float32))
    ang = pos.astype(jnp.float32)[:, None] * inv[None, :]
    cos, sin = jnp.cos(ang).astype(x.dtype), jnp.sin(ang).astype(x.dtype)
    xp = x.reshape(x.shape[:-1] + (half, 2))
    x1, x2 = xp[..., 0], xp[..., 1]
    out = jnp.stack([x1 * cos - x2 * sin, x2 * cos + x1 * sin], axis=-1)
    return out.reshape(x.shape)


def dilated_branch(q, k, v, window, dilation):
    B, H, Sp, Dh = q.shape
    steps = window // dilation
    L = Sp // dilation
    nb = L // steps

    def gather_stride(t):
        t = t.reshape(B, H, L, dilation, Dh).transpose(0, 1, 3, 2, 4)
        return t.reshape(B, H, dilation, nb, steps, Dh)

    def with_prev(t):
        prev = jnp.pad(t, ((0, 0), (0, 0), (0, 0), (1, 0), (0, 0), (0, 0)))[:, :, :, :-1]
        return jnp.concatenate([prev, t], axis=4)

    qb = gather_stride(q)
    kb = with_prev(gather_stride(k))
    vb = with_prev(gather_stride(v))
    s = jnp.einsum('bhrnqd,bhrnkd->bhrnqk', qb, kb,
                   preferred_element_type=jnp.float32) * (Dh ** -0.5)
    qi = jnp.arange(steps)[:, None]
    kj = jnp.arange(2 * steps)[None, :]
    dist = steps + qi - kj
    band = (dist >= 0) & (dist <= steps)
    has_prev = (jnp.arange(nb) > 0)[:, None, None]
    mask = band[None] & (has_prev | (kj >= steps)[None])
    s = jnp.where(mask, s, jnp.finfo(jnp.float32).min)
    m = jnp.max(s, axis=-1, keepdims=True)
    p = jnp.exp(s - m)
    den = jnp.sum(p, axis=-1, keepdims=True)
    o = jnp.einsum('bhrnqk,bhrnkd->bhrnqd', p.astype(v.dtype), vb,
                   preferred_element_type=jnp.float32) / den
    lse = (m + jnp.log(den))[..., 0]
    o = o.reshape(B, H, dilation, L, Dh).transpose(0, 1, 3, 2, 4).reshape(B, H, Sp, Dh)
    lse = lse.reshape(B, H, dilation, L).transpose(0, 1, 3, 2).reshape(B, H, Sp)
    return o, lse


def dilated_attention(q, k, v):
    B, H, S, Dh = q.shape
    Sp = -(-S // PAD_MULT) * PAD_MULT
    pad = ((0, 0), (0, 0), (0, Sp - S), (0, 0))
    pos = jnp.arange(Sp)
    qp = rope_half(jnp.pad(q, pad), pos)
    kp = rope_half(jnp.pad(k, pad), pos)
    vp = jnp.pad(v, pad)
    outs, lses = [], []
    for window, dilation in DILATED_BRANCHES:
        o, lse = dilated_branch(qp, kp, vp, window, dilation)
        outs.append(o)
        lses.append(lse)
    w = jax.nn.softmax(jnp.stack(lses, axis=0), axis=0)
    o = jnp.sum(w[..., None] * jnp.stack(outs, axis=0), axis=0)
    return o[:, :, :S].astype(q.dtype)


def retention(q, k, v):
    B, H, S, Dk = q.shape
    Dv = v.shape[-1]
    C = RET_CHUNK
    N = S // C
    pos = jnp.arange(S)
    q = retnet_rotate(q, pos).astype(jnp.float32)
    k = (retnet_rotate(k, pos) * (Dk ** -0.5)).astype(jnp.float32)
    v = v.astype(jnp.float32)
    log_gamma = jnp.log1p(-(2.0 ** (-5.0 - jnp.arange(H, dtype=jnp.float32))))
    cpos = jnp.arange(C, dtype=jnp.float32)
    diff = cpos[:, None] - cpos[None, :]
    decay = jnp.where(diff[None] >= 0, jnp.exp(jnp.maximum(diff, 0.0)[None] * log_gamma[:, None, None]), 0.0)
    qc = q.reshape(B, H, N, C, Dk)
    kc = k.reshape(B, H, N, C, Dk)
    vc = v.reshape(B, H, N, C, Dv)
    s = jnp.einsum('bhnid,bhnjd->bhnij', qc, kc) * decay[None, :, None]
    o_intra = jnp.einsum('bhnij,bhnjv->bhniv', s, vc)
    k_dec = jnp.exp((C - 1 - cpos)[None, :] * log_gamma[:, None])
    q_dec = jnp.exp((cpos + 1)[None, :] * log_gamma[:, None])
    chunk_decay = jnp.exp(C * log_gamma)[None, :, None, None]
    kv = jnp.einsum('bhnjd,hj,bhnjv->nbhdv', kc, k_dec, vc)

    def step(state, kv_n):
        return state * chunk_decay + kv_n, state

    _, state_prev = lax.scan(step, jnp.zeros((B, H, Dk, Dv), jnp.float32), kv)
    o_inter = jnp.einsum('bhnid,hi,nbhdv->bhniv', qc, q_dec, state_prev)
    o = (o_intra + o_inter).reshape(B, H, S, Dv)
    mu = jnp.mean(o, axis=-1, keepdims=True)
    var = jnp.mean(jnp.square(o - mu), axis=-1, keepdims=True)
    return (o - mu) * lax.rsqrt(var + GN_EPS)


def split_heads(t, n_heads):
    B, S, W = t.shape
    return t.reshape(B, S, n_heads, W // n_heads).transpose(0, 2, 1, 3)


def merge_heads(t):
    B, H, S, Dh = t.shape
    return t.transpose(0, 2, 1, 3).reshape(B, S, H * Dh)


def hybrid_mixer(h, w_in, ret_gn_gain, w_out):
    proj = h @ w_in
    cuts = [int(c) for c in np.cumsum(IN_SPLITS)[:-1]]
    qa, ka, va, ga, qr, kr, vr, gr = jnp.split(proj, cuts, axis=-1)
    attn = merge_heads(dilated_attention(split_heads(qa, ATTN_HEADS),
                                         split_heads(ka, ATTN_HEADS),
                                         split_heads(va, ATTN_HEADS)))
    ret = merge_heads(retention(split_heads(qr, RET_HEADS),
                                split_heads(kr, RET_HEADS),
                                split_heads(vr, RET_HEADS)))
    ret = (ret * ret_gn_gain.astype(jnp.float32)).astype(h.dtype)
    mixed = jnp.concatenate([jax.nn.silu(ga) * attn, jax.nn.silu(gr) * ret], axis=-1)
    return mixed @ w_out


def setup_inputs(seed: int = 0) -> dict:
    key = jax.random.key(seed)
    ks = jax.random.split(key, 7)
    x = jax.random.normal(ks[0], (BATCH, SEQ, D_MODEL), jnp.float32)
    norm_gain = 1.0 + 0.02 * jax.random.normal(ks[1], (DEPTH, D_MODEL), jnp.float32)
    w_in = jax.random.normal(ks[2], (DEPTH, D_MODEL, IN_WIDTH), jnp.float32) * D_MODEL ** -0.5
    ret_gn_gain = 1.0 + 0.02 * jax.random.normal(ks[3], (DEPTH, RET_V_WIDTH), jnp.float32)
    w_out = jax.random.normal(ks[4], (DEPTH, MIX_WIDTH, D_MODEL), jnp.float32) * MIX_WIDTH ** -0.5
    final_gain = 1.0 + 0.02 * jax.random.normal(ks[5], (D_MODEL,), jnp.float32)
    return {"x": x, "norm_gain": norm_gain, "w_in": w_in, "ret_gn_gain": ret_gn_gain,
            "w_out": w_out, "final_gain": final_gain}


def reference(x, norm_gain, w_in, ret_gn_gain, w_out, final_gain):
    for layer in range(DEPTH):
        h = rms_norm(x, norm_gain[layer])
        x = x + hybrid_mixer(h, w_in[layer], ret_gn_gain[layer], w_out[layer])
    return rms_norm(x, final_gain)
```

```python
import functools

import numpy as np
import jax
import jax.numpy as jnp
from jax import lax
from jax.experimental import pallas as pl
from jax.experimental.pallas import tpu as pltpu

LANES = 128
F32_SUBLANES = 8

ATTN_HEADS = 8
ATTN_HEAD_DIM = 64
ATTN_WIDTH = ATTN_HEADS * ATTN_HEAD_DIM
RET_HEADS = 4
RET_KEY_DIM = 64
RET_VALUE_DIM = 128
RET_QK_WIDTH = RET_HEADS * RET_KEY_DIM
RET_V_WIDTH = RET_HEADS * RET_VALUE_DIM
RET_CHUNK = 128
ROPE_THETA = 10000.0
NORM_EPS = 1e-6
GN_EPS = 1e-5

MAX_DILATION = 16
BLOCK = 128
NEG_BIG = -1e30

VMEM_LIMIT_BYTES = 48 * 1024 * 1024
RESIDUES_PER_STEP = 2


def _silu(g):
    return g / (1.0 + jnp.exp(-g))


def _proj_kernel(x_ref, gain_ref, w_ref, ca_ref, sa_ref, cr_ref, sr_ref,
                 qa_ref, ka_ref, va_ref, ga_ref, qr_ref, kr_ref, vr_ref, gr_ref,
                 *, rows, d_model):
    lane = lax.broadcasted_iota(jnp.int32, (rows, LANES), 1)
    first_half = (lane % ATTN_HEAD_DIM) < (ATTN_HEAD_DIM // 2)
    even = (lane % 2) == 0
    gain = gain_ref[...]

    def rope_half(slab, cos, sin):
        fwd = pltpu.roll(slab, LANES - ATTN_HEAD_DIM // 2, 1)
        bwd = pltpu.roll(slab, ATTN_HEAD_DIM // 2, 1)
        return slab * cos + jnp.where(first_half, fwd, bwd) * sin

    def rope_pairs(slab, cos, sin):
        fwd = pltpu.roll(slab, LANES - 1, 1)
        bwd = pltpu.roll(slab, 1, 1)
        return slab * cos + jnp.where(even, fwd, bwd) * sin

    def proj(h, lo, width):
        return jnp.dot(h, w_ref[:, lo:lo + width], preferred_element_type=jnp.float32)

    for rr in range(RESIDUES_PER_STEP):
        x = x_ref[0, :, rr * d_model:(rr + 1) * d_model]
        ms = jnp.mean(x * x, axis=-1, keepdims=True)
        h = (x * lax.rsqrt(ms + NORM_EPS) * gain).astype(jnp.bfloat16)
        rs = slice(rr * rows, (rr + 1) * rows)
        ca, sa = ca_ref[rs, :], sa_ref[rs, :]
        cr, sr = cr_ref[rs, :], sr_ref[rs, :]

        lo = 0
        pq = proj(h, lo, ATTN_WIDTH); lo += ATTN_WIDTH
        pk = proj(h, lo, ATTN_WIDTH); lo += ATTN_WIDTH
        pv = proj(h, lo, ATTN_WIDTH); lo += ATTN_WIDTH
        pg = proj(h, lo, ATTN_WIDTH); lo += ATTN_WIDTH
        for hp in range(ATTN_WIDTH // LANES):
            ls = slice(hp * LANES, (hp + 1) * LANES)
            qa_ref[0, hp, rr] = rope_half(pq[:, ls], ca, sa) * (ATTN_HEAD_DIM ** -0.5)
            ka_ref[0, hp, rr] = rope_half(pk[:, ls], ca, sa)
            va_ref[0, hp, rr] = pv[:, ls]
            ga_ref[0, hp, rr] = _silu(pg[:, ls]).astype(jnp.bfloat16)

        pq = proj(h, lo, RET_QK_WIDTH); lo += RET_QK_WIDTH
        pk = proj(h, lo, RET_QK_WIDTH); lo += RET_QK_WIDTH
        for hp in range(RET_QK_WIDTH // LANES):
            ls = slice(hp * LANES, (hp + 1) * LANES)
            os_ = slice(rr * RET_QK_WIDTH + hp * LANES, rr * RET_QK_WIDTH + (hp + 1) * LANES)
            qr_ref[0, :, os_] = rope_pairs(pq[:, ls], cr, sr).astype(jnp.bfloat16)
            kr_ref[0, :, os_] = (rope_pairs(pk[:, ls], cr, sr) * (RET_KEY_DIM ** -0.5)).astype(jnp.bfloat16)
        pv = proj(h, lo, RET_V_WIDTH); lo += RET_V_WIDTH
        pg = proj(h, lo, RET_V_WIDTH); lo += RET_V_WIDTH
        os_ = slice(rr * RET_V_WIDTH, (rr + 1) * RET_V_WIDTH)
        vr_ref[0, :, os_] = pv.astype(jnp.bfloat16)
        gr_ref[0, :, os_] = _silu(pg).astype(jnp.bfloat16)


def _attn_kernel(q_ref, k_ref, v_ref, g_ref, bias_ref, o_ref, acc_s, m_s, den_s, kv_s,
                 *, rows_per_residue):
    lane = lax.broadcasted_iota(jnp.int32, (BLOCK, LANES), 1)
    head0 = lane < ATTN_HEAD_DIM

    def gather(ref, offs, size):
        parts = [ref[pl.ds(o, size), :] for o in offs]
        return parts[0] if len(parts) == 1 else jnp.concatenate(parts, axis=0)

    def scatter(ref, offs, size, val):
        for i, o in enumerate(offs):
            ref[pl.ds(o, size), :] = val[i * size:(i + 1) * size, :]

    def advance_keys(n, offs, size):
        @pl.when(n == 0)
        def _():
            kv_s[:, 0:BLOCK, :] = jnp.zeros((2, BLOCK, LANES), jnp.bfloat16)

        @pl.when(n > 0)
        def _():
            kv_s[:, 0:BLOCK, :] = kv_s[:, BLOCK:2 * BLOCK, :]

        kv_s[0, BLOCK:2 * BLOCK, :] = gather(k_ref, offs, size).astype(jnp.bfloat16)
        kv_s[1, BLOCK:2 * BLOCK, :] = gather(v_ref, offs, size).astype(jnp.bfloat16)

    def block_softmax(branch, n, offs, size):
        advance_keys(n, offs, size)
        q = gather(q_ref, offs, size).astype(jnp.bfloat16)
        bias = bias_ref[branch, jnp.where(n == 0, 1, 0)]
        keys = kv_s[0]
        vals = kv_s[1]
        res = []
        for h in range(2):
            qh = jnp.where(head0 if h == 0 else jnp.logical_not(head0), q, jnp.zeros_like(q))
            s = lax.dot_general(qh, keys, (((1,), (1,)), ((), ())),
                                preferred_element_type=jnp.float32) + bias
            m = jnp.max(s, axis=1, keepdims=True)
            p = jnp.exp(s - m)
            den = jnp.sum(p, axis=1, keepdims=True)
            acc = jnp.dot(p.astype(jnp.bfloat16), vals, preferred_element_type=jnp.float32)
            res.append((acc, m, den))
        acc = jnp.where(head0, res[0][0], res[1][0])
        m = jnp.where(head0, res[0][1], res[1][1])
        den = jnp.where(head0, res[0][2], res[1][2])
        return acc, m, den

    def merge(offs, size, acc, m, den):
        m_old = gather(m_s, offs, size)
        m_new = jnp.maximum(m_old, m)
        w_old = jnp.exp(m_old - m_new)
        w_new = jnp.exp(m - m_new)
        acc = w_old * gather(acc_s, offs, size) + w_new * acc
        den = w_old * gather(den_s, offs, size) + w_new * den
        return acc, m_new, den

    rpr = rows_per_residue

    def branch1(n, carry):
        size = rpr // (rpr * MAX_DILATION // BLOCK)
        offs = [pl.multiple_of(rpr * r + size * n, F32_SUBLANES) for r in range(MAX_DILATION)]
        acc, m, den = block_softmax(0, n, offs, size)
        scatter(acc_s, offs, size, acc)
        scatter(m_s, offs, size, m)
        scatter(den_s, offs, size, den)
        return carry

    lax.fori_loop(0, rpr * MAX_DILATION // BLOCK, branch1, 0)

    blocks2 = rpr * 4 // BLOCK

    def branch2(i, carry):
        c = i // blocks2
        n = i % blocks2
        size = BLOCK // 4
        offs = [pl.multiple_of(rpr * (4 * a + c) + size * n, size) for a in range(4)]
        acc, m, den = block_softmax(1, n, offs, size)
        acc, m, den = merge(offs, size, acc, m, den)
        scatter(acc_s, offs, size, acc)
        scatter(m_s, offs, size, m)
        scatter(den_s, offs, size, den)
        return carry

    lax.fori_loop(0, 4 * blocks2, branch2, 0)

    blocks3 = rpr // BLOCK

    def branch3(i, carry):
        r = i // blocks3
        n = i % blocks3
        offs = [pl.multiple_of(rpr * r + BLOCK * n, BLOCK)]
        acc, m, den = block_softmax(2, n, offs, BLOCK)
        acc, m, den = merge(offs, BLOCK, acc, m, den)
        gate = g_ref[pl.ds(offs[0], BLOCK), :].astype(jnp.float32)
        o_ref[pl.ds(offs[0], BLOCK), :] = (acc / den * gate).astype(o_ref.dtype)
        return carry

    lax.fori_loop(0, MAX_DILATION * blocks3, branch3, 0)


def _ret_kernel(q_ref, k_ref, v_ref, g_ref, gain_ref, dmat_ref, qdec_ref, kdec_ref, cdec_ref, bmask_ref,
                o_ref, state_s, *, n_chunks):
    C = RET_CHUNK
    lane = lax.broadcasted_iota(jnp.int32, (C, LANES), 1)
    head0 = lane < RET_KEY_DIM
    state_s[...] = jnp.zeros_like(state_s)

    def chunk(n, carry):
        off = pl.multiple_of(n * C, C)
        q = q_ref[pl.ds(off, C), :]
        k = k_ref[pl.ds(off, C), :]
        v = v_ref[pl.ds(off, C), :]
        state = state_s[...]
        o_inter = jnp.dot(q, state.astype(jnp.bfloat16), preferred_element_type=jnp.float32) * qdec_ref[0]
        outs = []
        for h in range(2):
            qh = jnp.where(head0 if h == 0 else jnp.logical_not(head0), q, jnp.zeros_like(q))
            s = lax.dot_general(qh, k, (((1,), (1,)), ((), ())),
                                preferred_element_type=jnp.float32) * dmat_ref[0, h]
            vs = slice(h * RET_VALUE_DIM, (h + 1) * RET_VALUE_DIM)
            o = jnp.dot(s.astype(jnp.bfloat16), v[:, vs], preferred_element_type=jnp.float32) + o_inter[:, vs]
            mu = jnp.mean(o, axis=-1, keepdims=True)
            d = o - mu
            var = jnp.mean(d * d, axis=-1, keepdims=True)
            y = d * lax.rsqrt(var + GN_EPS) * gain_ref[:, vs]
            outs.append(y * g_ref[pl.ds(off, C), vs].astype(jnp.float32))
        o_ref[pl.ds(off, C), :] = jnp.concatenate(outs, axis=1).astype(o_ref.dtype)
        kd = (k.astype(jnp.float32) * kdec_ref[0]).astype(jnp.bfloat16)
        kv = lax.dot_general(kd, v, (((0,), (0,)), ((), ())), preferred_element_type=jnp.float32)
        state_s[...] = state * cdec_ref[0] + kv * bmask_ref[...]
        return carry

    lax.fori_loop(0, n_chunks, chunk, 0)


def _out_kernel(x_ref, a_ref, r_ref, w_ref, gain_ref, o_ref, *, d_model):
    gain = gain_ref[...]
    for rr in range(RESIDUES_PER_STEP):
        x = x_ref[0, :, rr * d_model:(rr + 1) * d_model]
        mixed = jnp.concatenate(
            [a_ref[0, hp, rr] for hp in range(ATTN_WIDTH // LANES)]
            + [r_ref[0, :, rr * RET_V_WIDTH:(rr + 1) * RET_V_WIDTH]], axis=1)
        y = x + jnp.dot(mixed, w_ref[...], preferred_element_type=jnp.float32)
        ms = jnp.mean(y * y, axis=-1, keepdims=True)
        o_ref[0, :, rr * d_model:(rr + 1) * d_model] = y * lax.rsqrt(ms + NORM_EPS) * gain


def _rotation_tables(seq):
    rpr = seq // MAX_DILATION
    p = jnp.arange(seq)
    pos = (MAX_DILATION * (p % rpr) + p // rpr).astype(jnp.float32)
    half = ATTN_HEAD_DIM // 2
    inv = ROPE_THETA ** (-jnp.arange(half, dtype=jnp.float32) / half)
    ang = pos[:, None] * inv[None, :]
    cos, sin = jnp.cos(ang), jnp.sin(ang)
    ca = jnp.tile(jnp.concatenate([cos, cos], axis=1), (1, LANES // ATTN_HEAD_DIM))
    sa = jnp.tile(jnp.concatenate([-sin, sin], axis=1), (1, LANES // ATTN_HEAD_DIM))
    half = RET_KEY_DIM // 2
    inv = 1.0 / (ROPE_THETA ** jnp.linspace(0.0, 1.0, half, dtype=jnp.float32))
    ang = pos[:, None] * inv[None, :]
    cos, sin = jnp.cos(ang), jnp.sin(ang)
    cr = jnp.tile(jnp.repeat(cos, 2, axis=1), (1, LANES // RET_KEY_DIM))
    sr = jnp.tile(jnp.stack([-sin, sin], axis=-1).reshape(seq, RET_KEY_DIM), (1, LANES // RET_KEY_DIM))
    return ca, sa, cr, sr


def _attention_bias():
    i = np.arange(BLOCK)
    orders = [
        16 * (i % 8) + i // 8,
        4 * (i % 32) + i // 32,
        i,
    ]
    out = np.zeros((3, 2, BLOCK, 2 * BLOCK), np.float32)
    for b, u in enumerate(orders):
        prev_ok = u[None, :] >= u[:, None]
        cur_ok = u[None, :] <= u[:, None]
        out[b, 0] = np.where(np.concatenate([prev_ok, cur_ok], axis=1), 0.0, NEG_BIG)
        out[b, 1] = np.where(np.concatenate([np.zeros_like(prev_ok), cur_ok], axis=1), 0.0, NEG_BIG)
    return jnp.asarray(out)


def _retention_tables():
    C = RET_CHUNK
    log_gamma = jnp.log1p(-(2.0 ** (-5.0 - jnp.arange(RET_HEADS, dtype=jnp.float32))))
    cpos = jnp.arange(C, dtype=jnp.float32)
    diff = cpos[:, None] - cpos[None, :]
    dmat = jnp.where(diff[None] >= 0, jnp.exp(jnp.maximum(diff, 0.0)[None] * log_gamma[:, None, None]), 0.0)
    k_dec = jnp.exp((C - 1 - cpos)[None, :] * log_gamma[:, None])
    q_dec = jnp.exp((cpos + 1)[None, :] * log_gamma[:, None])
    chunk_decay = jnp.exp(C * log_gamma)
    pairs = RET_HEADS // 2
    dmat = dmat.reshape(pairs, 2, C, C)
    qdec = jnp.repeat(q_dec.reshape(pairs, 2, C).transpose(0, 2, 1), RET_VALUE_DIM, axis=2)
    kdec = jnp.repeat(k_dec.reshape(pairs, 2, C).transpose(0, 2, 1), RET_KEY_DIM, axis=2)
    row_head = jnp.arange(2 * RET_KEY_DIM) // RET_KEY_DIM
    col_head = jnp.arange(2 * RET_VALUE_DIM) // RET_VALUE_DIM
    bmask = (row_head[:, None] == col_head[None, :]).astype(jnp.float32)
    cdec = bmask[None] * chunk_decay.reshape(pairs, 2)[:, row_head][:, :, None]
    return dmat, qdec, kdec, cdec, bmask


def _mixer_layer(x, norm_gain, w_in, ret_gn_gain, w_out, out_gain, apply_out_norm):
    B, S, D = x.shape
    rpr = S // MAX_DILATION
    R = RESIDUES_PER_STEP
    n_steps = MAX_DILATION // R
    pairs_a = ATTN_WIDTH // LANES
    f32, bf16 = jnp.float32, jnp.bfloat16
    params = functools.partial(pltpu.CompilerParams, vmem_limit_bytes=VMEM_LIMIT_BYTES)

    ca, sa, cr, sr = _rotation_tables(S)
    x_res = x.reshape(B, rpr, MAX_DILATION * D)
    tab_spec = pl.BlockSpec((R * rpr, LANES), lambda j, b: (j, 0))
    attn_spec = pl.BlockSpec((1, pairs_a, R, rpr, LANES), lambda j, b: (b, 0, j, 0, 0))
    attn_shape = (B, pairs_a, MAX_DILATION, rpr, LANES)

    def nat_spec(width):
        return pl.BlockSpec((1, rpr, R * width), lambda j, b: (b, 0, j))

    qa, ka, va, ga, qr, kr, vr, gr = pl.pallas_call(
        functools.partial(_proj_kernel, rows=rpr, d_model=D),
        grid=(n_steps, B),
        in_specs=[
            nat_spec(D),
            pl.BlockSpec((1, D), lambda j, b: (0, 0)),
            pl.BlockSpec(w_in.shape, lambda j, b: (0, 0)),
            tab_spec, tab_spec, tab_spec, tab_spec,
        ],
        out_specs=[attn_spec, attn_spec, attn_spec, attn_spec,
                   nat_spec(RET_QK_WIDTH), nat_spec(RET_QK_WIDTH), nat_spec(RET_V_WIDTH), nat_spec(RET_V_WIDTH)],
        out_shape=[
            jax.ShapeDtypeStruct(attn_shape, f32), jax.ShapeDtypeStruct(attn_shape, f32),
            jax.ShapeDtypeStruct(attn_shape, f32), jax.ShapeDtypeStruct(attn_shape, bf16),
            jax.ShapeDtypeStruct((B, rpr, MAX_DILATION * RET_QK_WIDTH), bf16),
            jax.ShapeDtypeStruct((B, rpr, MAX_DILATION * RET_QK_WIDTH), bf16),
            jax.ShapeDtypeStruct((B, rpr, MAX_DILATION * RET_V_WIDTH), bf16),
            jax.ShapeDtypeStruct((B, rpr, MAX_DILATION * RET_V_WIDTH), bf16),
        ],
        compiler_params=params(dimension_semantics=("arbitrary", "arbitrary")),
        name="proj_rope",
    )(x_res, norm_gain.reshape(1, D), w_in.astype(bf16), ca, sa, cr, sr)

    seq_spec = pl.BlockSpec((None, None, S, LANES), lambda b, hp: (b, hp, 0, 0))
    flat = (B, pairs_a, S, LANES)
    attn = pl.pallas_call(
        functools.partial(_attn_kernel, rows_per_residue=rpr),
        grid=(B, pairs_a),
        in_specs=[seq_spec, seq_spec, seq_spec, seq_spec,
                  pl.BlockSpec((3, 2, BLOCK, 2 * BLOCK), lambda b, hp: (0, 0, 0, 0))],
        out_specs=seq_spec,
        out_shape=jax.ShapeDtypeStruct(flat, bf16),
        scratch_shapes=[pltpu.VMEM((S, LANES), f32), pltpu.VMEM((S, LANES), f32), pltpu.VMEM((S, LANES), f32),
                        pltpu.VMEM((2, 2 * BLOCK, LANES), bf16)],
        compiler_params=params(dimension_semantics=("arbitrary", "arbitrary")),
        name="dilated_attention",
    )(qa.reshape(flat), ka.reshape(flat), va.reshape(flat), ga.reshape(flat), _attention_bias())

    dmat, qdec, kdec, cdec, bmask = _retention_tables()
    pairs_r = RET_HEADS // 2
    C = RET_CHUNK
    qk_spec = pl.BlockSpec((None, S, 2 * RET_KEY_DIM), lambda b, hp: (b, 0, hp))
    v_spec = pl.BlockSpec((None, S, 2 * RET_VALUE_DIM), lambda b, hp: (b, 0, hp))
    ret = pl.pallas_call(
        functools.partial(_ret_kernel, n_chunks=S // C),
        grid=(B, pairs_r),
        in_specs=[qk_spec, qk_spec, v_spec, v_spec,
                  pl.BlockSpec((1, 2 * RET_VALUE_DIM), lambda b, hp: (0, hp)),
                  pl.BlockSpec((1, 2, C, C), lambda b, hp: (hp, 0, 0, 0)),
                  pl.BlockSpec((1, C, 2 * RET_VALUE_DIM), lambda b, hp: (hp, 0, 0)),
                  pl.BlockSpec((1, C, 2 * RET_KEY_DIM), lambda b, hp: (hp, 0, 0)),
                  pl.BlockSpec((1, 2 * RET_KEY_DIM, 2 * RET_VALUE_DIM), lambda b, hp: (hp, 0, 0)),
                  pl.BlockSpec((2 * RET_KEY_DIM, 2 * RET_VALUE_DIM), lambda b, hp: (0, 0))],
        out_specs=v_spec,
        out_shape=jax.ShapeDtypeStruct((B, S, RET_V_WIDTH), bf16),
        scratch_shapes=[pltpu.VMEM((2 * RET_KEY_DIM, 2 * RET_VALUE_DIM), f32)],
        compiler_params=params(dimension_semantics=("arbitrary", "arbitrary")),
        name="retention",
    )(qr.reshape(B, S, RET_QK_WIDTH), kr.reshape(B, S, RET_QK_WIDTH),
      vr.reshape(B, S, RET_V_WIDTH), gr.reshape(B, S, RET_V_WIDTH),
      ret_gn_gain.reshape(1, RET_V_WIDTH), dmat, qdec, kdec, cdec, bmask)

    out = pl.pallas_call(
        functools.partial(_out_kernel, d_model=D),
        grid=(n_steps, B),
        in_specs=[nat_spec(D), attn_spec, nat_spec(RET_V_WIDTH),
                  pl.BlockSpec(w_out.shape, lambda j, b: (0, 0)),
                  pl.BlockSpec((1, D), lambda j, b: (0, 0))],
        out_specs=nat_spec(D),
        out_shape=jax.ShapeDtypeStruct((B, rpr, MAX_DILATION * D), f32),
        compiler_params=params(dimension_semantics=("arbitrary", "arbitrary")),
        name="out_proj_norm",
    )(x_res, attn.reshape(attn_shape), ret.reshape(B, rpr, MAX_DILATION * RET_V_WIDTH),
      w_out.astype(bf16), out_gain.reshape(1, D))
    return out.reshape(B, S, D)


def kernel(x, norm_gain, w_in, ret_gn_gain, w_out, final_gain):
    depth = norm_gain.shape[0]
    assert depth == 1, "the fused output projection + final norm assumes a single layer"
    assert x.shape[1] % (MAX_DILATION * BLOCK) == 0
    return _mixer_layer(x, norm_gain[0], w_in[0], ret_gn_gain[0], w_out[0], final_gain, True)
```

```python
import functools

import numpy as np
import jax
import jax.numpy as jnp
from jax import lax
from jax.experimental import pallas as pl
from jax.experimental.pallas import tpu as pltpu

LANES = 128

ATTN_HEADS = 8
ATTN_HEAD_DIM = 64
ATTN_WIDTH = ATTN_HEADS * ATTN_HEAD_DIM
ATTN_PAIRS = ATTN_WIDTH // LANES
RET_HEADS = 4
RET_KEY_DIM = 64
RET_VALUE_DIM = 128
RET_QK_WIDTH = RET_HEADS * RET_KEY_DIM
RET_V_WIDTH = RET_HEADS * RET_VALUE_DIM
RET_CHUNK = 128
ROPE_THETA = 10000.0
NORM_EPS = 1e-6
GN_EPS = 1e-5

CLASSES = 4
BLOCK = 128
MAX_DILATION = 16
NEG_BIG = -1e30

VMEM_LIMIT_BYTES = 48 * 1024 * 1024
ROW_TILE = 512
BLOCKS_PER_STEP = 4


def _silu(g):
    return g / (1.0 + jnp.exp(-g))


def _proj_kernel(x_ref, gain_ref, w_ref, ca_ref, sa_ref, cr_ref, sr_ref,
                 qa_ref, ka_ref, va_ref, ga_ref, qr_ref, kr_ref, vr_ref, gr_ref, perm_s):
    rows = x_ref.shape[1]
    lane = lax.broadcasted_iota(jnp.int32, (rows, LANES), 1)
    first_half = (lane % ATTN_HEAD_DIM) < (ATTN_HEAD_DIM // 2)
    even = (lane % 2) == 0

    def rope_half(slab, cos, sin):
        fwd = pltpu.roll(slab, LANES - ATTN_HEAD_DIM // 2, 1)
        bwd = pltpu.roll(slab, ATTN_HEAD_DIM // 2, 1)
        return slab * cos + jnp.where(first_half, fwd, bwd) * sin

    def rope_pairs(slab, cos, sin):
        fwd = pltpu.roll(slab, LANES - 1, 1)
        bwd = pltpu.roll(slab, 1, 1)
        return slab * cos + jnp.where(even, fwd, bwd) * sin

    def proj(h, lo, width):
        return jnp.dot(h, w_ref[:, lo:lo + width], preferred_element_type=jnp.float32)

    def store_class_major(out_ref, hp, slot, slab):
        perm_s[slot] = slab
        for c in range(CLASSES):
            out_ref[0, hp, c] = perm_s[slot, pl.ds(c, rows // CLASSES, stride=CLASSES), :]

    x = x_ref[0]
    ms = jnp.mean(x * x, axis=-1, keepdims=True)
    h = (x * lax.rsqrt(ms + NORM_EPS) * gain_ref[...]).astype(jnp.bfloat16)
    ca, sa = ca_ref[...], sa_ref[...]
    cr, sr = cr_ref[...], sr_ref[...]

    lo = 0
    pq = proj(h, lo, ATTN_WIDTH); lo += ATTN_WIDTH
    pk = proj(h, lo, ATTN_WIDTH); lo += ATTN_WIDTH
    pv = proj(h, lo, ATTN_WIDTH); lo += ATTN_WIDTH
    pg = proj(h, lo, ATTN_WIDTH); lo += ATTN_WIDTH
    for hp in range(ATTN_PAIRS):
        ls = slice(hp * LANES, (hp + 1) * LANES)
        store_class_major(qa_ref, hp, hp, rope_half(pq[:, ls], ca, sa) * (ATTN_HEAD_DIM ** -0.5))
        store_class_major(ka_ref, hp, ATTN_PAIRS + hp, rope_half(pk[:, ls], ca, sa))
        store_class_major(va_ref, hp, 2 * ATTN_PAIRS + hp, pv[:, ls])
    ga_ref[0] = _silu(pg).astype(jnp.bfloat16)

    pq = proj(h, lo, RET_QK_WIDTH); lo += RET_QK_WIDTH
    pk = proj(h, lo, RET_QK_WIDTH); lo += RET_QK_WIDTH
    for hp in range(RET_QK_WIDTH // LANES):
        ls = slice(hp * LANES, (hp + 1) * LANES)
        qr_ref[0, :, ls] = rope_pairs(pq[:, ls], cr, sr).astype(jnp.bfloat16)
        kr_ref[0, :, ls] = (rope_pairs(pk[:, ls], cr, sr) * (RET_KEY_DIM ** -0.5)).astype(jnp.bfloat16)
    pv = proj(h, lo, RET_V_WIDTH); lo += RET_V_WIDTH
    pg = proj(h, lo, RET_V_WIDTH); lo += RET_V_WIDTH
    vr_ref[0] = pv.astype(jnp.bfloat16)
    gr_ref[0] = _silu(pg).astype(jnp.bfloat16)


def _attn_kernel(q_ref, k_ref, v_ref, bias_ref, o_ref, acc_s, m_s, den_s):
    G = BLOCKS_PER_STEP
    L = q_ref.shape[1]
    lane = lax.broadcasted_iota(jnp.int32, (BLOCK, LANES), 1)
    head0 = lane < ATTN_HEAD_DIM
    bf16 = jnp.bfloat16

    def softmax_block(q, k_prev, k_cur, v_prev, v_cur, bias):
        keys = jnp.concatenate([k_prev, k_cur], axis=0)
        vals = jnp.concatenate([v_prev, v_cur], axis=0)
        res = []
        for h in range(2):
            qh = jnp.where(head0 if h == 0 else jnp.logical_not(head0), q, jnp.zeros_like(q))
            s = lax.dot_general(qh, keys, (((1,), (1,)), ((), ())),
                                preferred_element_type=jnp.float32) + bias
            m = jnp.max(s, axis=1, keepdims=True)
            p = jnp.exp(s - m)
            den = jnp.sum(p, axis=1, keepdims=True)
            acc = jnp.dot(p.astype(bf16), vals, preferred_element_type=jnp.float32)
            res.append((acc, m, den))
        acc = jnp.where(head0, res[0][0], res[1][0])
        m = jnp.where(head0, res[0][1], res[1][1])
        den = jnp.where(head0, res[0][2], res[1][2])
        return acc, m, den

    def merge(old, new):
        acc_o, m_o, den_o = old
        acc_n, m_n, den_n = new
        m = jnp.maximum(m_o, m_n)
        w_o = jnp.exp(m_o - m)
        w_n = jnp.exp(m_n - m)
        return w_o * acc_o + w_n * acc_n, m, w_o * den_o + w_n * den_n

    def run_sequence(branch, n0, n_blocks, load, first):
        prev = jnp.maximum(n0 - 1, 0)
        ids = [prev] + [n0 + g for g in range(n_blocks)]
        kb = [load(k_ref, nb).astype(bf16) for nb in ids]
        vb = [load(v_ref, nb).astype(bf16) for nb in ids]
        out = []
        for g in range(n_blocks):
            if g > 0:
                bias = bias_ref[branch, 0]
            elif isinstance(first, bool):
                bias = bias_ref[branch, 1 if first else 0]
            else:
                bias = bias_ref[branch, jnp.where(first, 1, 0)]
            q = load(q_ref, n0 + g).astype(bf16)
            out.append(softmax_block(q, kb[g], kb[g + 1], vb[g], vb[g + 1], bias))
        return out

    state = (acc_s, m_s, den_s)

    def load1(ref, n):
        start = pl.multiple_of(n * (BLOCK // CLASSES), BLOCK // CLASSES)
        return jnp.concatenate([ref[c, pl.ds(start, BLOCK // CLASSES), :] for c in range(CLASSES)], axis=0)

    def store1(ref, n, val):
        start = pl.multiple_of(n * (BLOCK // CLASSES), BLOCK // CLASSES)
        sz = BLOCK // CLASSES
        for c in range(CLASSES):
            ref[c, pl.ds(start, sz), :] = val[c * sz:(c + 1) * sz, :]

    def branch1(i, carry):
        n0 = i * G
        res = run_sequence(0, n0, G, load1, n0 == 0)
        for g in range(G):
            for ref, val in zip(state, res[g]):
                store1(ref, n0 + g, val)
        return carry

    lax.fori_loop(0, CLASSES * L // BLOCK // G, branch1, 0)

    per_class = L // BLOCK // G

    def branch2(i, carry):
        c = i // per_class
        n0 = (i % per_class) * G

        def load2(ref, n):
            return ref[c, pl.ds(pl.multiple_of(n * BLOCK, BLOCK), BLOCK), :]

        res = run_sequence(1, n0, G, load2, n0 == 0)
        old = [tuple(load2(ref, n0 + g) for ref in state) for g in range(G)]
        new = [merge(old[g], res[g]) for g in range(G)]
        for g in range(G):
            for ref, val in zip(state, new[g]):
                ref[c, pl.ds(pl.multiple_of((n0 + g) * BLOCK, BLOCK), BLOCK), :] = val
        return carry

    lax.fori_loop(0, CLASSES * per_class, branch2, 0)

    blocks3 = L // (MAX_DILATION // CLASSES) // BLOCK
    res_per_iter = G // blocks3

    def branch3(i, carry):
        items = []
        for j in range(res_per_iter):
            r = i * res_per_iter + j
            c = r % CLASSES
            a = r // CLASSES

            def load3(ref, n, c=c, a=a):
                return ref[c, pl.ds(n * (BLOCK * MAX_DILATION // CLASSES) + a, BLOCK, stride=MAX_DILATION // CLASSES), :]

            res = run_sequence(2, 0, blocks3, load3, True)
            items.append((c, load3, res))
        merged = []
        for c, load3, res in items:
            old = [tuple(load3(ref, n) for ref in state) for n in range(blocks3)]
            merged.append([merge(old[n], res[n]) for n in range(blocks3)])
        for (c, load3, res), new, j in zip(items, merged, range(res_per_iter)):
            a = (i * res_per_iter + j) // CLASSES
            for n in range(blocks3):
                acc, _, den = new[n]
                o_ref[c, pl.ds(n * (BLOCK * MAX_DILATION // CLASSES) + a, BLOCK,
                               stride=MAX_DILATION // CLASSES), :] = acc / den
        return carry

    lax.fori_loop(0, MAX_DILATION // res_per_iter, branch3, 0)


def _ret_kernel(q_ref, k_ref, v_ref, g_ref, gain_ref, dmat_ref, qdec_ref, kdec_ref, cdec_ref, bmask_ref,
                o_ref, state_s, *, n_chunks):
    C = RET_CHUNK
    lane = lax.broadcasted_iota(jnp.int32, (C, LANES), 1)
    head0 = lane < RET_KEY_DIM
    state_s[...] = jnp.zeros_like(state_s)

    def chunk(n, carry):
        off = pl.multiple_of(n * C, C)
        q = q_ref[pl.ds(off, C), :]
        k = k_ref[pl.ds(off, C), :]
        v = v_ref[pl.ds(off, C), :]
        state = state_s[...]
        o_inter = jnp.dot(q, state.astype(jnp.bfloat16), preferred_element_type=jnp.float32) * qdec_ref[0]
        outs = []
        for h in range(2):
            qh = jnp.where(head0 if h == 0 else jnp.logical_not(head0), q, jnp.zeros_like(q))
            s = lax.dot_general(qh, k, (((1,), (1,)), ((), ())),
                                preferred_element_type=jnp.float32) * dmat_ref[0, h]
            vs = slice(h * RET_VALUE_DIM, (h + 1) * RET_VALUE_DIM)
            o = jnp.dot(s.astype(jnp.bfloat16), v[:, vs], preferred_element_type=jnp.float32) + o_inter[:, vs]
            mu = jnp.mean(o, axis=-1, keepdims=True)
            d = o - mu
            var = jnp.mean(d * d, axis=-1, keepdims=True)
            y = d * lax.rsqrt(var + GN_EPS) * gain_ref[:, vs]
            outs.append(y * g_ref[pl.ds(off, C), vs].astype(jnp.float32))
        o_ref[pl.ds(off, C), :] = jnp.concatenate(outs, axis=1).astype(o_ref.dtype)
        kd = (k.astype(jnp.float32) * kdec_ref[0]).astype(jnp.bfloat16)
        kv = lax.dot_general(kd, v, (((0,), (0,)), ((), ())), preferred_element_type=jnp.float32)
        state_s[...] = state * cdec_ref[0] + kv * bmask_ref[...]
        return carry

    lax.fori_loop(0, n_chunks, chunk, 0)


def _out_kernel(x_ref, a_ref, ga_ref, r_ref, w_ref, gain_ref, o_ref, perm_s):
    rows = x_ref.shape[1]
    for hp in range(ATTN_PAIRS):
        for c in range(CLASSES):
            perm_s[hp, pl.ds(c, rows // CLASSES, stride=CLASSES), :] = a_ref[0, hp, c]
    attn = jnp.concatenate([perm_s[hp] for hp in range(ATTN_PAIRS)], axis=1)
    gated = (attn * ga_ref[0].astype(jnp.float32)).astype(jnp.bfloat16)
    mixed = jnp.concatenate([gated, r_ref[0]], axis=1)
    y = x_ref[0] + jnp.dot(mixed, w_ref[...], preferred_element_type=jnp.float32)
    ms = jnp.mean(y * y, axis=-1, keepdims=True)
    o_ref[0] = y * lax.rsqrt(ms + NORM_EPS) * gain_ref[...]


def _rotation_tables(seq):
    pos = jnp.arange(seq).astype(jnp.float32)
    half = ATTN_HEAD_DIM // 2
    inv = ROPE_THETA ** (-jnp.arange(half, dtype=jnp.float32) / half)
    ang = pos[:, None] * inv[None, :]
    cos, sin = jnp.cos(ang), jnp.sin(ang)
    ca = jnp.tile(jnp.concatenate([cos, cos], axis=1), (1, LANES // ATTN_HEAD_DIM))
    sa = jnp.tile(jnp.concatenate([-sin, sin], axis=1), (1, LANES // ATTN_HEAD_DIM))
    half = RET_KEY_DIM // 2
    inv = 1.0 / (ROPE_THETA ** jnp.linspace(0.0, 1.0, half, dtype=jnp.float32))
    ang = pos[:, None] * inv[None, :]
    cos, sin = jnp.cos(ang), jnp.sin(ang)
    cr = jnp.tile(jnp.repeat(cos, 2, axis=1), (1, LANES // RET_KEY_DIM))
    sr = jnp.tile(jnp.stack([-sin, sin], axis=-1).reshape(seq, RET_KEY_DIM), (1, LANES // RET_KEY_DIM))
    return ca, sa, cr, sr


def _attention_bias():
    i = np.arange(BLOCK)
    sz = BLOCK // CLASSES
    orders = [
        CLASSES * (i % sz) + i // sz,
        i,
        i,
    ]
    out = np.zeros((3, 2, BLOCK, 2 * BLOCK), np.float32)
    for b, u in enumerate(orders):
        prev_ok = u[None, :] >= u[:, None]
        cur_ok = u[None, :] <= u[:, None]
        out[b, 0] = np.where(np.concatenate([prev_ok, cur_ok], axis=1), 0.0, NEG_BIG)
        out[b, 1] = np.where(np.concatenate([np.zeros_like(prev_ok), cur_ok], axis=1), 0.0, NEG_BIG)
    return jnp.asarray(out)


def _retention_tables():
    C = RET_CHUNK
    log_gamma = jnp.log1p(-(2.0 ** (-5.0 - jnp.arange(RET_HEADS, dtype=jnp.float32))))
    cpos = jnp.arange(C, dtype=jnp.float32)
    diff = cpos[:, None] - cpos[None, :]
    dmat = jnp.where(diff[None] >= 0, jnp.exp(jnp.maximum(diff, 0.0)[None] * log_gamma[:, None, None]), 0.0)
    k_dec = jnp.exp((C - 1 - cpos)[None, :] * log_gamma[:, None])
    q_dec = jnp.exp((cpos + 1)[None, :] * log_gamma[:, None])
    chunk_decay = jnp.exp(C * log_gamma)
    pairs = RET_HEADS // 2
    dmat = dmat.reshape(pairs, 2, C, C)
    qdec = jnp.repeat(q_dec.reshape(pairs, 2, C).transpose(0, 2, 1), RET_VALUE_DIM, axis=2)
    kdec = jnp.repeat(k_dec.reshape(pairs, 2, C).transpose(0, 2, 1), RET_KEY_DIM, axis=2)
    row_head = jnp.arange(2 * RET_KEY_DIM) // RET_KEY_DIM
    col_head = jnp.arange(2 * RET_VALUE_DIM) // RET_VALUE_DIM
    bmask = (row_head[:, None] == col_head[None, :]).astype(jnp.float32)
    cdec = bmask[None] * chunk_decay.reshape(pairs, 2)[:, row_head][:, :, None]
    return dmat, qdec, kdec, cdec, bmask


def _mixer_layer(x, norm_gain, w_in, ret_gn_gain, w_out, out_gain):
    B, S, D = x.shape
    L = S // CLASSES
    TR = ROW_TILE
    f32, bf16 = jnp.float32, jnp.bfloat16
    params = functools.partial(pltpu.CompilerParams, vmem_limit_bytes=VMEM_LIMIT_BYTES)
    two_arb = ("arbitrary", "arbitrary")

    ca, sa, cr, sr = _rotation_tables(S)
    tab_spec = pl.BlockSpec((TR, LANES), lambda m, b: (m, 0))
    cm_shape = (B, ATTN_PAIRS, CLASSES, L, LANES)
    cm_spec = pl.BlockSpec((1, ATTN_PAIRS, CLASSES, TR // CLASSES, LANES), lambda m, b: (b, 0, 0, m, 0))

    def nat_spec(width):
        return pl.BlockSpec((1, TR, width), lambda m, b: (b, m, 0))

    def whole(shape):
        return pl.BlockSpec(shape, lambda m, b: (0,) * len(shape))

    qa, ka, va, ga, qr, kr, vr, gr = pl.pallas_call(
        _proj_kernel,
        grid=(S // TR, B),
        in_specs=[nat_spec(D), whole((1, D)), whole(w_in.shape), tab_spec, tab_spec, tab_spec, tab_spec],
        out_specs=[cm_spec, cm_spec, cm_spec, nat_spec(ATTN_WIDTH),
                   nat_spec(RET_QK_WIDTH), nat_spec(RET_QK_WIDTH), nat_spec(RET_V_WIDTH), nat_spec(RET_V_WIDTH)],
        out_shape=[
            jax.ShapeDtypeStruct(cm_shape, f32), jax.ShapeDtypeStruct(cm_shape, f32),
            jax.ShapeDtypeStruct(cm_shape, f32), jax.ShapeDtypeStruct((B, S, ATTN_WIDTH), bf16),
            jax.ShapeDtypeStruct((B, S, RET_QK_WIDTH), bf16), jax.ShapeDtypeStruct((B, S, RET_QK_WIDTH), bf16),
            jax.ShapeDtypeStruct((B, S, RET_V_WIDTH), bf16), jax.ShapeDtypeStruct((B, S, RET_V_WIDTH), bf16),
        ],
        scratch_shapes=[pltpu.VMEM((3 * ATTN_PAIRS, TR, LANES), f32)],
        compiler_params=params(dimension_semantics=two_arb),
        name="proj_rope",
    )(x, norm_gain.reshape(1, D), w_in.astype(bf16), ca, sa, cr, sr)

    seq_spec = pl.BlockSpec((None, None, CLASSES, L, LANES), lambda b, hp: (b, hp, 0, 0, 0))
    attn = pl.pallas_call(
        _attn_kernel,
        grid=(B, ATTN_PAIRS),
        in_specs=[seq_spec, seq_spec, seq_spec,
                  pl.BlockSpec((3, 2, BLOCK, 2 * BLOCK), lambda b, hp: (0, 0, 0, 0))],
        out_specs=seq_spec,
        out_shape=jax.ShapeDtypeStruct(cm_shape, f32),
        scratch_shapes=[pltpu.VMEM((CLASSES, L, LANES), f32)] * 3,
        compiler_params=params(dimension_semantics=two_arb),
        name="dilated_attention",
    )(qa, ka, va, _attention_bias())

    dmat, qdec, kdec, cdec, bmask = _retention_tables()
    pairs_r = RET_HEADS // 2
    C = RET_CHUNK
    qk_spec = pl.BlockSpec((None, S, 2 * RET_KEY_DIM), lambda b, hp: (b, 0, hp))
    v_spec = pl.BlockSpec((None, S, 2 * RET_VALUE_DIM), lambda b, hp: (b, 0, hp))
    ret = pl.pallas_call(
        functools.partial(_ret_kernel, n_chunks=S // C),
        grid=(B, pairs_r),
        in_specs=[qk_spec, qk_spec, v_spec, v_spec,
                  pl.BlockSpec((1, 2 * RET_VALUE_DIM), lambda b, hp: (0, hp)),
                  pl.BlockSpec((1, 2, C, C), lambda b, hp: (hp, 0, 0, 0)),
                  pl.BlockSpec((1, C, 2 * RET_VALUE_DIM), lambda b, hp: (hp, 0, 0)),
                  pl.BlockSpec((1, C, 2 * RET_KEY_DIM), lambda b, hp: (hp, 0, 0)),
                  pl.BlockSpec((1, 2 * RET_KEY_DIM, 2 * RET_VALUE_DIM), lambda b, hp: (hp, 0, 0)),
                  pl.BlockSpec((2 * RET_KEY_DIM, 2 * RET_VALUE_DIM), lambda b, hp: (0, 0))],
        out_specs=v_spec,
        out_shape=jax.ShapeDtypeStruct((B, S, RET_V_WIDTH), bf16),
        scratch_shapes=[pltpu.VMEM((2 * RET_KEY_DIM, 2 * RET_VALUE_DIM), f32)],
        compiler_params=params(dimension_semantics=two_arb),
        name="retention",
    )(qr, kr, vr, gr, ret_gn_gain.reshape(1, RET_V_WIDTH), dmat, qdec, kdec, cdec, bmask)

    return pl.pallas_call(
        _out_kernel,
        grid=(S // TR, B),
        in_specs=[nat_spec(D), cm_spec, nat_spec(ATTN_WIDTH), nat_spec(RET_V_WIDTH),
                  whole(w_out.shape), whole((1, D))],
        out_specs=nat_spec(D),
        out_shape=jax.ShapeDtypeStruct((B, S, D), f32),
        scratch_shapes=[pltpu.VMEM((ATTN_PAIRS, TR, LANES), f32)],
        compiler_params=params(dimension_semantics=two_arb),
        name="out_proj_norm",
    )(x, attn, ga, ret, w_out.astype(bf16), out_gain.reshape(1, D))


def kernel(x, norm_gain, w_in, ret_gn_gain, w_out, final_gain):
    assert norm_gain.shape[0] == 1, "the fused output projection + final norm assumes a single layer"
    assert x.shape[1] % (MAX_DILATION * BLOCK) == 0
    return _mixer_layer(x, norm_gain[0], w_in[0], ret_gn_gain[0], w_out[0], final_gain)
```

```python
import functools
import math

import numpy as np
import jax
import jax.numpy as jnp
from jax import lax
from jax.experimental import pallas as pl
from jax.experimental.pallas import tpu as pltpu

LANES = 128

ATTN_HEADS = 8
ATTN_HEAD_DIM = 64
ATTN_WIDTH = ATTN_HEADS * ATTN_HEAD_DIM
ATTN_PAIRS = ATTN_WIDTH // LANES
RET_HEADS = 4
RET_KEY_DIM = 64
RET_VALUE_DIM = 128
RET_QK_WIDTH = RET_HEADS * RET_KEY_DIM
RET_V_WIDTH = RET_HEADS * RET_VALUE_DIM
RET_CHUNK = 128
ROPE_THETA = 10000.0
NORM_EPS = 1e-6
GN_EPS = 1e-5

CLASSES = 4
BLOCK = 128
MAX_DILATION = 16
NEG_BIG = -1e30

VMEM_LIMIT_BYTES = 48 * 1024 * 1024
ROW_TILE = 512
BLOCKS_PER_STEP = 4


def _silu(g):
    return g / (1.0 + jnp.exp(-g))


def _proj_kernel(x_ref, gain_ref, w_ref, ca_ref, sa_ref, cr_ref, sr_ref,
                 qa_ref, ka_ref, va_ref, ga_ref, qr_ref, kr_ref, vr_ref, gr_ref, perm_s):
    rows = x_ref.shape[1]
    lane = lax.broadcasted_iota(jnp.int32, (rows, LANES), 1)
    first_half = (lane % ATTN_HEAD_DIM) < (ATTN_HEAD_DIM // 2)
    even = (lane % 2) == 0

    def rope_half(slab, cos, sin):
        fwd = pltpu.roll(slab, LANES - ATTN_HEAD_DIM // 2, 1)
        bwd = pltpu.roll(slab, ATTN_HEAD_DIM // 2, 1)
        return slab * cos + jnp.where(first_half, fwd, bwd) * sin

    def rope_pairs(slab, cos, sin):
        fwd = pltpu.roll(slab, LANES - 1, 1)
        bwd = pltpu.roll(slab, 1, 1)
        return slab * cos + jnp.where(even, fwd, bwd) * sin

    def proj(h, lo, width):
        return jnp.dot(h, w_ref[:, lo:lo + width], preferred_element_type=jnp.float32)

    def store_class_major(out_ref, hp, slot, slab):
        perm_s[slot] = slab
        for c in range(CLASSES):
            out_ref[0, hp, c] = perm_s[slot, pl.ds(c, rows // CLASSES, stride=CLASSES), :]

    x = x_ref[0]
    ms = jnp.mean(x * x, axis=-1, keepdims=True)
    h = (x * lax.rsqrt(ms + NORM_EPS) * gain_ref[...]).astype(jnp.bfloat16)
    ca, sa = ca_ref[...], sa_ref[...]
    cr, sr = cr_ref[...], sr_ref[...]

    lo = 0
    pq = proj(h, lo, ATTN_WIDTH); lo += ATTN_WIDTH
    pk = proj(h, lo, ATTN_WIDTH); lo += ATTN_WIDTH
    pv = proj(h, lo, ATTN_WIDTH); lo += ATTN_WIDTH
    pg = proj(h, lo, ATTN_WIDTH); lo += ATTN_WIDTH
    q_scale = math.log2(math.e) * ATTN_HEAD_DIM ** -0.5
    for hp in range(ATTN_PAIRS):
        ls = slice(hp * LANES, (hp + 1) * LANES)
        store_class_major(qa_ref, hp, hp, rope_half(pq[:, ls], ca, sa) * q_scale)
        store_class_major(ka_ref, hp, ATTN_PAIRS + hp, rope_half(pk[:, ls], ca, sa))
        store_class_major(va_ref, hp, 2 * ATTN_PAIRS + hp, pv[:, ls])
    ga_ref[0] = _silu(pg).astype(jnp.bfloat16)

    pq = proj(h, lo, RET_QK_WIDTH); lo += RET_QK_WIDTH
    pk = proj(h, lo, RET_QK_WIDTH); lo += RET_QK_WIDTH
    for hp in range(RET_QK_WIDTH // LANES):
        ls = slice(hp * LANES, (hp + 1) * LANES)
        qr_ref[0, :, ls] = rope_pairs(pq[:, ls], cr, sr).astype(jnp.bfloat16)
        kr_ref[0, :, ls] = (rope_pairs(pk[:, ls], cr, sr) * (RET_KEY_DIM ** -0.5)).astype(jnp.bfloat16)
    pv = proj(h, lo, RET_V_WIDTH); lo += RET_V_WIDTH
    pg = proj(h, lo, RET_V_WIDTH); lo += RET_V_WIDTH
    vr_ref[0] = pv.astype(jnp.bfloat16)
    gr_ref[0] = _silu(pg).astype(jnp.bfloat16)


def _attn_kernel(q_ref, k_ref, v_ref, bias_ref, ones_ref, o_ref,
                 acc_s, m_s, den_s, s_s, p_s, mrow_s):
    G = BLOCKS_PER_STEP
    L = q_ref.shape[1]
    lane = lax.broadcasted_iota(jnp.int32, (BLOCK, LANES), 1)
    head0 = lane < ATTN_HEAD_DIM
    head_masks = (head0, jnp.logical_not(head0))
    bf16 = jnp.bfloat16
    state = (acc_s, m_s, den_s)
    sub = BLOCK // CLASSES
    quad = MAX_DILATION // CLASSES
    blocks3 = L // quad // BLOCK

    def is_static(v):
        return isinstance(v, (int, bool))

    def segments(branch, t):
        if branch == 0:
            def load(ref, n):
                start = n * sub if is_static(n) else pl.multiple_of(n * sub, sub)
                return jnp.concatenate([ref[c, pl.ds(start, sub), :] for c in range(CLASSES)], axis=0)

            def store(ref, n, val):
                start = n * sub if is_static(n) else pl.multiple_of(n * sub, sub)
                for c in range(CLASSES):
                    ref[c, pl.ds(start, sub), :] = val[c * sub:(c + 1) * sub, :]

            n0 = t * G
            prev = max(n0 - 1, 0) if is_static(n0) else jnp.maximum(n0 - 1, 0)
            return [(load, store, [n0 + g for g in range(G)], prev, n0 == 0)]
        if branch == 1:
            per_class = L // BLOCK // G
            c = t // per_class
            n0 = (t % per_class) * G

            def load(ref, n):
                start = n * BLOCK if is_static(n) else pl.multiple_of(n * BLOCK, BLOCK)
                return ref[c, pl.ds(start, BLOCK), :]

            def store(ref, n, val):
                start = n * BLOCK if is_static(n) else pl.multiple_of(n * BLOCK, BLOCK)
                ref[c, pl.ds(start, BLOCK), :] = val

            prev = max(n0 - 1, 0) if is_static(n0) else jnp.maximum(n0 - 1, 0)
            return [(load, store, [n0 + g for g in range(G)], prev, n0 == 0)]
        segs = []
        for j in range(G // blocks3):
            r = t * (G // blocks3) + j
            c = r % CLASSES
            a = r // CLASSES

            def load(ref, n, c=c, a=a):
                return ref[c, pl.ds(n * BLOCK * quad + a, BLOCK, stride=quad), :]

            def store(ref, n, val, c=c, a=a):
                ref[c, pl.ds(n * BLOCK * quad + a, BLOCK, stride=quad), :] = val

            segs.append((load, store, list(range(blocks3)), 0, True))
        return segs

    n_groups = (CLASSES * L // BLOCK // G, CLASSES * L // BLOCK // G, MAX_DILATION * blocks3 // G)

    def stage_scores(branch, t, slot):
        i = 0
        for load, _, blocks, prev, first in segments(branch, t):
            kb = [load(k_ref, nb).astype(bf16) for nb in [prev] + blocks]
            for g, n in enumerate(blocks):
                if g > 0:
                    bias = bias_ref[branch, 0]
                elif is_static(first):
                    bias = bias_ref[branch, 1 if first else 0]
                else:
                    bias = bias_ref[branch, jnp.where(first, 1, 0)]
                keys = jnp.concatenate([kb[g], kb[g + 1]], axis=0)
                q = load(q_ref, n).astype(bf16)
                for h in range(2):
                    qh = jnp.where(head_masks[h], q, jnp.zeros_like(q))
                    s = lax.dot_general(qh, keys, (((1,), (1,)), ((), ())),
                                        preferred_element_type=jnp.float32) + bias
                    s_s[slot, i, h] = s
                    mrow_s[slot, i, h] = jnp.broadcast_to(jnp.max(s, axis=1, keepdims=True), (BLOCK, LANES))
                i += 1

    def stage_softmax(slot):
        for i in range(G):
            for h in range(2):
                m = mrow_s[slot, i, h]
                p_s[slot, i, h] = jnp.exp2(s_s[slot, i, h] - jnp.concatenate([m, m], axis=1)).astype(bf16)

    def stage_values(branch, t, slot):
        i = 0
        new = []
        for load, store, blocks, prev, _ in segments(branch, t):
            vb = [load(v_ref, nb).astype(bf16) for nb in [prev] + blocks]
            for g, n in enumerate(blocks):
                vals = jnp.concatenate([jnp.concatenate([vb[g], vb[g + 1]], axis=0), ones_ref[...]], axis=1)
                r = [jnp.dot(p_s[slot, i, h], vals, preferred_element_type=jnp.float32) for h in range(2)]
                acc = jnp.where(head0, r[0][:, :LANES], r[1][:, :LANES])
                den = jnp.where(head0, r[0][:, LANES:], r[1][:, LANES:])
                m = jnp.where(head0, mrow_s[slot, i, 0], mrow_s[slot, i, 1])
                new.append((load, store, n, acc, m, den))
                i += 1
        if branch == 0:
            for load, store, n, acc, m, den in new:
                for ref, val in zip(state, (acc, m, den)):
                    store(ref, n, val)
            return
        old = [tuple(load(ref, n) for ref in state) for load, _, n, _, _, _ in new]
        merged = []
        for (acc_o, m_o, den_o), (_, _, _, acc_n, m_n, den_n) in zip(old, new):
            m = jnp.maximum(m_o, m_n)
            w_o = jnp.exp2(m_o - m)
            w_n = jnp.exp2(m_n - m)
            merged.append((w_o * acc_o + w_n * acc_n, m, w_o * den_o + w_n * den_n))
        for (_, store, n, _, _, _), (acc, m, den) in zip(new, merged):
            if branch == 2:
                store(o_ref, n, acc / den)
            else:
                for ref, val in zip(state, (acc, m, den)):
                    store(ref, n, val)

    per_branch = n_groups[0]
    assert n_groups == (per_branch,) * 3 and per_branch % 2 == 0 and per_branch > 2
    total = 3 * per_branch

    def step(branch, t, parity):
        def at(delta):
            if is_static(t):
                g = branch * per_branch + t - delta
                return None if not 0 <= g < total else (g // per_branch, g % per_branch)
            return branch, t - delta

        vg, sg, cg = at(2), at(1), at(0)
        if vg is not None:
            stage_values(vg[0], vg[1], parity)
        if sg is not None:
            stage_softmax(1 - parity)
        if cg is not None:
            stage_scores(cg[0], cg[1], parity)

    def two_steps(u, carry, branch):
        step(branch, 2 * u, 0)
        step(branch, 2 * u + 1, 1)
        return carry

    for branch in range(3):
        step(branch, 0, 0)
        step(branch, 1, 1)
        lax.fori_loop(1, per_branch // 2, functools.partial(two_steps, branch=branch), 0)
    step(3, 0, 0)
    step(3, 1, 1)


def _ret_kernel(q_ref, k_ref, v_ref, g_ref, gain_ref, dmat_ref, qdec_ref, kdec_ref, cdec_ref, bmask_ref,
                o_ref, state_s, *, n_chunks):
    C = RET_CHUNK
    lane = lax.broadcasted_iota(jnp.int32, (C, LANES), 1)
    head0 = lane < RET_KEY_DIM
    state_s[...] = jnp.zeros_like(state_s)

    def chunk(n, carry):
        off = pl.multiple_of(n * C, C)
        q = q_ref[pl.ds(off, C), :]
        k = k_ref[pl.ds(off, C), :]
        v = v_ref[pl.ds(off, C), :]
        state = state_s[...]
        o_inter = jnp.dot(q, state.astype(jnp.bfloat16), preferred_element_type=jnp.float32) * qdec_ref[0]
        outs = []
        for h in range(2):
            qh = jnp.where(head0 if h == 0 else jnp.logical_not(head0), q, jnp.zeros_like(q))
            s = lax.dot_general(qh, k, (((1,), (1,)), ((), ())),
                                preferred_element_type=jnp.float32) * dmat_ref[0, h]
            vs = slice(h * RET_VALUE_DIM, (h + 1) * RET_VALUE_DIM)
            o = jnp.dot(s.astype(jnp.bfloat16), v[:, vs], preferred_element_type=jnp.float32) + o_inter[:, vs]
            mu = jnp.mean(o, axis=-1, keepdims=True)
            d = o - mu
            var = jnp.mean(d * d, axis=-1, keepdims=True)
            y = d * lax.rsqrt(var + GN_EPS) * gain_ref[:, vs]
            outs.append(y * g_ref[pl.ds(off, C), vs].astype(jnp.float32))
        o_ref[pl.ds(off, C), :] = jnp.concatenate(outs, axis=1).astype(o_ref.dtype)
        kd = (k.astype(jnp.float32) * kdec_ref[0]).astype(jnp.bfloat16)
        kv = lax.dot_general(kd, v, (((0,), (0,)), ((), ())), preferred_element_type=jnp.float32)
        state_s[...] = state * cdec_ref[0] + kv * bmask_ref[...]
        return carry

    lax.fori_loop(0, n_chunks, chunk, 0)


def _out_kernel(x_ref, a_ref, ga_ref, r_ref, w_ref, gain_ref, o_ref, perm_s):
    rows = x_ref.shape[1]
    for hp in range(ATTN_PAIRS):
        for c in range(CLASSES):
            perm_s[hp, pl.ds(c, rows // CLASSES, stride=CLASSES), :] = a_ref[0, hp, c]
    attn = jnp.concatenate([perm_s[hp] for hp in range(ATTN_PAIRS)], axis=1)
    gated = (attn * ga_ref[0].astype(jnp.float32)).astype(jnp.bfloat16)
    mixed = jnp.concatenate([gated, r_ref[0]], axis=1)
    y = x_ref[0] + jnp.dot(mixed, w_ref[...], preferred_element_type=jnp.float32)
    ms = jnp.mean(y * y, axis=-1, keepdims=True)
    o_ref[0] = y * lax.rsqrt(ms + NORM_EPS) * gain_ref[...]


def _rotation_tables(seq):
    pos = jnp.arange(seq).astype(jnp.float32)
    half = ATTN_HEAD_DIM // 2
    inv = ROPE_THETA ** (-jnp.arange(half, dtype=jnp.float32) / half)
    ang = pos[:, None] * inv[None, :]
    cos, sin = jnp.cos(ang), jnp.sin(ang)
    ca = jnp.tile(jnp.concatenate([cos, cos], axis=1), (1, LANES // ATTN_HEAD_DIM))
    sa = jnp.tile(jnp.concatenate([-sin, sin], axis=1), (1, LANES // ATTN_HEAD_DIM))
    half = RET_KEY_DIM // 2
    inv = 1.0 / (ROPE_THETA ** jnp.linspace(0.0, 1.0, half, dtype=jnp.float32))
    ang = pos[:, None] * inv[None, :]
    cos, sin = jnp.cos(ang), jnp.sin(ang)
    cr = jnp.tile(jnp.repeat(cos, 2, axis=1), (1, LANES // RET_KEY_DIM))
    sr = jnp.tile(jnp.stack([-sin, sin], axis=-1).reshape(seq, RET_KEY_DIM), (1, LANES // RET_KEY_DIM))
    return ca, sa, cr, sr


def _attention_bias():
    i = np.arange(BLOCK)
    sz = BLOCK // CLASSES
    orders = [
        CLASSES * (i % sz) + i // sz,
        i,
        i,
    ]
    out = np.zeros((3, 2, BLOCK, 2 * BLOCK), np.float32)
    for b, u in enumerate(orders):
        prev_ok = u[None, :] >= u[:, None]
        cur_ok = u[None, :] <= u[:, None]
        out[b, 0] = np.where(np.concatenate([prev_ok, cur_ok], axis=1), 0.0, NEG_BIG)
        out[b, 1] = np.where(np.concatenate([np.zeros_like(prev_ok), cur_ok], axis=1), 0.0, NEG_BIG)
    return jnp.asarray(out)


def _retention_tables():
    C = RET_CHUNK
    log_gamma = jnp.log1p(-(2.0 ** (-5.0 - jnp.arange(RET_HEADS, dtype=jnp.float32))))
    cpos = jnp.arange(C, dtype=jnp.float32)
    diff = cpos[:, None] - cpos[None, :]
    dmat = jnp.where(diff[None] >= 0, jnp.exp(jnp.maximum(diff, 0.0)[None] * log_gamma[:, None, None]), 0.0)
    k_dec = jnp.exp((C - 1 - cpos)[None, :] * log_gamma[:, None])
    q_dec = jnp.exp((cpos + 1)[None, :] * log_gamma[:, None])
    chunk_decay = jnp.exp(C * log_gamma)
    pairs = RET_HEADS // 2
    dmat = dmat.reshape(pairs, 2, C, C)
    qdec = jnp.repeat(q_dec.reshape(pairs, 2, C).transpose(0, 2, 1), RET_VALUE_DIM, axis=2)
    kdec = jnp.repeat(k_dec.reshape(pairs, 2, C).transpose(0, 2, 1), RET_KEY_DIM, axis=2)
    row_head = jnp.arange(2 * RET_KEY_DIM) // RET_KEY_DIM
    col_head = jnp.arange(2 * RET_VALUE_DIM) // RET_VALUE_DIM
    bmask = (row_head[:, None] == col_head[None, :]).astype(jnp.float32)
    cdec = bmask[None] * chunk_decay.reshape(pairs, 2)[:, row_head][:, :, None]
    return dmat, qdec, kdec, cdec, bmask


def _mixer_layer(x, norm_gain, w_in, ret_gn_gain, w_out, out_gain):
    B, S, D = x.shape
    L = S // CLASSES
    TR = ROW_TILE
    G = BLOCKS_PER_STEP
    f32, bf16 = jnp.float32, jnp.bfloat16
    params = functools.partial(pltpu.CompilerParams, vmem_limit_bytes=VMEM_LIMIT_BYTES)
    two_arb = ("arbitrary", "arbitrary")

    ca, sa, cr, sr = _rotation_tables(S)
    tab_spec = pl.BlockSpec((TR, LANES), lambda m, b: (m, 0))
    cm_shape = (B, ATTN_PAIRS, CLASSES, L, LANES)
    cm_spec = pl.BlockSpec((1, ATTN_PAIRS, CLASSES, TR // CLASSES, LANES), lambda m, b: (b, 0, 0, m, 0))

    def nat_spec(width):
        return pl.BlockSpec((1, TR, width), lambda m, b: (b, m, 0))

    def whole(shape):
        return pl.BlockSpec(shape, lambda *_: (0,) * len(shape))

    qa, ka, va, ga, qr, kr, vr, gr = pl.pallas_call(
        _proj_kernel,
        grid=(S // TR, B),
        in_specs=[nat_spec(D), whole((1, D)), whole(w_in.shape), tab_spec, tab_spec, tab_spec, tab_spec],
        out_specs=[cm_spec, cm_spec, cm_spec, nat_spec(ATTN_WIDTH),
                   nat_spec(RET_QK_WIDTH), nat_spec(RET_QK_WIDTH), nat_spec(RET_V_WIDTH), nat_spec(RET_V_WIDTH)],
        out_shape=[
            jax.ShapeDtypeStruct(cm_shape, f32), jax.ShapeDtypeStruct(cm_shape, f32),
            jax.ShapeDtypeStruct(cm_shape, f32), jax.ShapeDtypeStruct((B, S, ATTN_WIDTH), bf16),
            jax.ShapeDtypeStruct((B, S, RET_QK_WIDTH), bf16), jax.ShapeDtypeStruct((B, S, RET_QK_WIDTH), bf16),
            jax.ShapeDtypeStruct((B, S, RET_V_WIDTH), bf16), jax.ShapeDtypeStruct((B, S, RET_V_WIDTH), bf16),
        ],
        scratch_shapes=[pltpu.VMEM((3 * ATTN_PAIRS, TR, LANES), f32)],
        compiler_params=params(dimension_semantics=two_arb),
        name="proj_rope",
    )(x, norm_gain.reshape(1, D), w_in.astype(bf16), ca, sa, cr, sr)

    seq_spec = pl.BlockSpec((None, None, CLASSES, L, LANES), lambda b, hp: (b, hp, 0, 0, 0))
    attn = pl.pallas_call(
        _attn_kernel,
        grid=(B, ATTN_PAIRS),
        in_specs=[seq_spec, seq_spec, seq_spec,
                  whole((3, 2, BLOCK, 2 * BLOCK)), whole((2 * BLOCK, LANES))],
        out_specs=seq_spec,
        out_shape=jax.ShapeDtypeStruct(cm_shape, f32),
        scratch_shapes=[pltpu.VMEM((CLASSES, L, LANES), f32)] * 3
        + [pltpu.VMEM((2, G, 2, BLOCK, 2 * BLOCK), f32), pltpu.VMEM((2, G, 2, BLOCK, 2 * BLOCK), bf16),
           pltpu.VMEM((2, G, 2, BLOCK, LANES), f32)],
        compiler_params=params(dimension_semantics=two_arb),
        name="dilated_attention",
    )(qa, ka, va, _attention_bias(), jnp.ones((2 * BLOCK, LANES), bf16))

    dmat, qdec, kdec, cdec, bmask = _retention_tables()
    pairs_r = RET_HEADS // 2
    C = RET_CHUNK
    qk_spec = pl.BlockSpec((None, S, 2 * RET_KEY_DIM), lambda b, hp: (b, 0, hp))
    v_spec = pl.BlockSpec((None, S, 2 * RET_VALUE_DIM), lambda b, hp: (b, 0, hp))
    ret = pl.pallas_call(
        functools.partial(_ret_kernel, n_chunks=S // C),
        grid=(B, pairs_r),
        in_specs=[qk_spec, qk_spec, v_spec, v_spec,
                  pl.BlockSpec((1, 2 * RET_VALUE_DIM), lambda b, hp: (0, hp)),
                  pl.BlockSpec((1, 2, C, C), lambda b, hp: (hp, 0, 0, 0)),
                  pl.BlockSpec((1, C, 2 * RET_VALUE_DIM), lambda b, hp: (hp, 0, 0)),
                  pl.BlockSpec((1, C, 2 * RET_KEY_DIM), lambda b, hp: (hp, 0, 0)),
                  pl.BlockSpec((1, 2 * RET_KEY_DIM, 2 * RET_VALUE_DIM), lambda b, hp: (hp, 0, 0)),
                  pl.BlockSpec((2 * RET_KEY_DIM, 2 * RET_VALUE_DIM), lambda b, hp: (0, 0))],
        out_specs=v_spec,
        out_shape=jax.ShapeDtypeStruct((B, S, RET_V_WIDTH), bf16),
        scratch_shapes=[pltpu.VMEM((2 * RET_KEY_DIM, 2 * RET_VALUE_DIM), f32)],
        compiler_params=params(dimension_semantics=two_arb),
        name="retention",
    )(qr, kr, vr, gr, ret_gn_gain.reshape(1, RET_V_WIDTH), dmat, qdec, kdec, cdec, bmask)

    return pl.pallas_call(
        _out_kernel,
        grid=(S // TR, B),
        in_specs=[nat_spec(D), cm_spec, nat_spec(ATTN_WIDTH), nat_spec(RET_V_WIDTH),
                  whole(w_out.shape), whole((1, D))],
        out_specs=nat_spec(D),
        out_shape=jax.ShapeDtypeStruct((B, S, D), f32),
        scratch_shapes=[pltpu.VMEM((ATTN_PAIRS, TR, LANES), f32)],
        compiler_params=params(dimension_semantics=two_arb),
        name="out_proj_norm",
    )(x, attn, ga, ret, w_out.astype(bf16), out_gain.reshape(1, D))


def kernel(x, norm_gain, w_in, ret_gn_gain, w_out, final_gain):
    assert norm_gain.shape[0] == 1, "the fused output projection + final norm assumes a single layer"
    assert x.shape[1] % (MAX_DILATION * BLOCK) == 0
    return _mixer_layer(x, norm_gain[0], w_in[0], ret_gn_gain[0], w_out[0], final_gain)
```

```python
import functools
import math

import numpy as np
import jax
import jax.numpy as jnp
from jax import lax
from jax.experimental import pallas as pl
from jax.experimental.pallas import tpu as pltpu

LANES = 128

ATTN_HEADS = 8
ATTN_HEAD_DIM = 64
ATTN_WIDTH = ATTN_HEADS * ATTN_HEAD_DIM
ATTN_PAIRS = ATTN_WIDTH // LANES
RET_HEADS = 4
RET_KEY_DIM = 64
RET_VALUE_DIM = 128
RET_QK_WIDTH = RET_HEADS * RET_KEY_DIM
RET_V_WIDTH = RET_HEADS * RET_VALUE_DIM
RET_CHUNK = 128
ROPE_THETA = 10000.0
NORM_EPS = 1e-6
GN_EPS = 1e-5

CLASSES = 4
BLOCK = 128
MAX_DILATION = 16
NEG_BIG = -1e30

VMEM_LIMIT_BYTES = 48 * 1024 * 1024
ROW_TILE = 512
BLOCKS_PER_STEP = 4
RET_CHUNKS_PER_STEP = 8


def _silu(g):
    return g / (1.0 + jnp.exp(-g))


def _proj_kernel(x_ref, gain_ref, w_ref, gn_gain_ref, ca_ref, sa_ref, cr_ref, sr_ref,
                 qa_ref, ka_ref, va_ref, ga_ref, qr_ref, kr_ref, vr_ref, gr_ref, perm_s):
    rows = x_ref.shape[1]
    lane = lax.broadcasted_iota(jnp.int32, (rows, LANES), 1)
    first_half = (lane % ATTN_HEAD_DIM) < (ATTN_HEAD_DIM // 2)
    even = (lane % 2) == 0

    def rope_half(slab, cos, sin):
        fwd = pltpu.roll(slab, LANES - ATTN_HEAD_DIM // 2, 1)
        bwd = pltpu.roll(slab, ATTN_HEAD_DIM // 2, 1)
        return slab * cos + jnp.where(first_half, fwd, bwd) * sin

    def rope_pairs(slab, cos, sin):
        fwd = pltpu.roll(slab, LANES - 1, 1)
        bwd = pltpu.roll(slab, 1, 1)
        return slab * cos + jnp.where(even, fwd, bwd) * sin

    def proj(h, lo, width):
        return jnp.dot(h, w_ref[:, lo:lo + width], preferred_element_type=jnp.float32)

    def store_class_major(out_ref, hp, slot, slab):
        perm_s[slot] = slab
        for c in range(CLASSES):
            out_ref[0, hp, c] = perm_s[slot, pl.ds(c, rows // CLASSES, stride=CLASSES), :]

    x = x_ref[0]
    ms = jnp.mean(x * x, axis=-1, keepdims=True)
    h = (x * lax.rsqrt(ms + NORM_EPS) * gain_ref[...]).astype(jnp.bfloat16)
    ca, sa = ca_ref[...], sa_ref[...]
    cr, sr = cr_ref[...], sr_ref[...]

    lo = 0
    pq = proj(h, lo, ATTN_WIDTH); lo += ATTN_WIDTH
    pk = proj(h, lo, ATTN_WIDTH); lo += ATTN_WIDTH
    pv = proj(h, lo, ATTN_WIDTH); lo += ATTN_WIDTH
    pg = proj(h, lo, ATTN_WIDTH); lo += ATTN_WIDTH
    q_scale = math.log2(math.e) * ATTN_HEAD_DIM ** -0.5
    for hp in range(ATTN_PAIRS):
        ls = slice(hp * LANES, (hp + 1) * LANES)
        store_class_major(qa_ref, hp, hp, rope_half(pq[:, ls], ca, sa) * q_scale)
        store_class_major(ka_ref, hp, ATTN_PAIRS + hp, rope_half(pk[:, ls], ca, sa))
        store_class_major(va_ref, hp, 2 * ATTN_PAIRS + hp, pv[:, ls])
    ga_ref[0] = _silu(pg).astype(jnp.bfloat16)

    pq = proj(h, lo, RET_QK_WIDTH); lo += RET_QK_WIDTH
    pk = proj(h, lo, RET_QK_WIDTH); lo += RET_QK_WIDTH
    for hp in range(RET_QK_WIDTH // LANES):
        ls = slice(hp * LANES, (hp + 1) * LANES)
        qr_ref[0, :, ls] = rope_pairs(pq[:, ls], cr, sr).astype(jnp.bfloat16)
        kr_ref[0, :, ls] = (rope_pairs(pk[:, ls], cr, sr) * (RET_KEY_DIM ** -0.5)).astype(jnp.bfloat16)
    pv = proj(h, lo, RET_V_WIDTH); lo += RET_V_WIDTH
    pg = proj(h, lo, RET_V_WIDTH); lo += RET_V_WIDTH
    vr_ref[0] = pv.astype(jnp.bfloat16)
    gr_ref[0] = (_silu(pg) * gn_gain_ref[...]).astype(jnp.bfloat16)


def _attn_kernel(q_ref, k_ref, v_ref, bias_ref, ones_ref, o_ref,
                 acc_s, m_s, den_s, s_s, p_s, mrow_s):
    G = BLOCKS_PER_STEP
    L = q_ref.shape[1]
    lane = lax.broadcasted_iota(jnp.int32, (BLOCK, LANES), 1)
    head0 = lane < ATTN_HEAD_DIM
    head_masks = (head0, jnp.logical_not(head0))
    bf16 = jnp.bfloat16
    state = (acc_s, m_s, den_s)
    sub = BLOCK // CLASSES
    quad = MAX_DILATION // CLASSES
    blocks3 = L // quad // BLOCK

    def is_static(v):
        return isinstance(v, (int, bool))

    def segments(branch, t):
        if branch == 0:
            def load(ref, n):
                start = n * sub if is_static(n) else pl.multiple_of(n * sub, sub)
                return jnp.concatenate([ref[c, pl.ds(start, sub), :] for c in range(CLASSES)], axis=0)

            def store(ref, n, val):
                start = n * sub if is_static(n) else pl.multiple_of(n * sub, sub)
                for c in range(CLASSES):
                    ref[c, pl.ds(start, sub), :] = val[c * sub:(c + 1) * sub, :]

            n0 = t * G
            prev = max(n0 - 1, 0) if is_static(n0) else jnp.maximum(n0 - 1, 0)
            return [(load, store, [n0 + g for g in range(G)], prev, n0 == 0)]
        if branch == 1:
            per_class = L // BLOCK // G
            c = t // per_class
            n0 = (t % per_class) * G

            def load(ref, n):
                start = n * BLOCK if is_static(n) else pl.multiple_of(n * BLOCK, BLOCK)
                return ref[c, pl.ds(start, BLOCK), :]

            def store(ref, n, val):
                start = n * BLOCK if is_static(n) else pl.multiple_of(n * BLOCK, BLOCK)
                ref[c, pl.ds(start, BLOCK), :] = val

            prev = max(n0 - 1, 0) if is_static(n0) else jnp.maximum(n0 - 1, 0)
            return [(load, store, [n0 + g for g in range(G)], prev, n0 == 0)]
        segs = []
        for j in range(G // blocks3):
            r = t * (G // blocks3) + j
            c = r % CLASSES
            a = r // CLASSES

            def load(ref, n, c=c, a=a):
                return ref[c, pl.ds(n * BLOCK * quad + a, BLOCK, stride=quad), :]

            def store(ref, n, val, c=c, a=a):
                ref[c, pl.ds(n * BLOCK * quad + a, BLOCK, stride=quad), :] = val

            segs.append((load, store, list(range(blocks3)), 0, True))
        return segs

    n_groups = (CLASSES * L // BLOCK // G, CLASSES * L // BLOCK // G, MAX_DILATION * blocks3 // G)

    def stage_scores(branch, t, slot):
        i = 0
        for load, _, blocks, prev, first in segments(branch, t):
            kb = [load(k_ref, nb).astype(bf16) for nb in [prev] + blocks]
            for g, n in enumerate(blocks):
                if g > 0:
                    bias = bias_ref[branch, 0]
                elif is_static(first):
                    bias = bias_ref[branch, 1 if first else 0]
                else:
                    bias = bias_ref[branch, jnp.where(first, 1, 0)]
                keys = jnp.concatenate([kb[g], kb[g + 1]], axis=0)
                q = load(q_ref, n).astype(bf16)
                for h in range(2):
                    qh = jnp.where(head_masks[h], q, jnp.zeros_like(q))
                    s = lax.dot_general(qh, keys, (((1,), (1,)), ((), ())),
                                        preferred_element_type=jnp.float32) + bias
                    s_s[slot, i, h] = s
                    mrow_s[slot, i, h] = jnp.broadcast_to(jnp.max(s, axis=1, keepdims=True), (BLOCK, LANES))
                i += 1

    def stage_softmax(slot):
        for i in range(G):
            for h in range(2):
                m = mrow_s[slot, i, h]
                p_s[slot, i, h] = jnp.exp2(s_s[slot, i, h] - jnp.concatenate([m, m], axis=1)).astype(bf16)

    def stage_values(branch, t, slot):
        i = 0
        new = []
        for load, store, blocks, prev, _ in segments(branch, t):
            vb = [load(v_ref, nb).astype(bf16) for nb in [prev] + blocks]
            for g, n in enumerate(blocks):
                vals = jnp.concatenate([jnp.concatenate([vb[g], vb[g + 1]], axis=0), ones_ref[...]], axis=1)
                r = [jnp.dot(p_s[slot, i, h], vals, preferred_element_type=jnp.float32) for h in range(2)]
                acc = jnp.where(head0, r[0][:, :LANES], r[1][:, :LANES])
                den = jnp.where(head0, r[0][:, LANES:], r[1][:, LANES:])
                m = jnp.where(head0, mrow_s[slot, i, 0], mrow_s[slot, i, 1])
                new.append((load, store, n, acc, m, den))
                i += 1
        if branch == 0:
            for load, store, n, acc, m, den in new:
                for ref, val in zip(state, (acc, m, den)):
                    store(ref, n, val)
            return
        old = [tuple(load(ref, n) for ref in state) for load, _, n, _, _, _ in new]
        merged = []
        for (acc_o, m_o, den_o), (_, _, _, acc_n, m_n, den_n) in zip(old, new):
            m = jnp.maximum(m_o, m_n)
            w_o = jnp.exp2(m_o - m)
            w_n = jnp.exp2(m_n - m)
            merged.append((w_o * acc_o + w_n * acc_n, m, w_o * den_o + w_n * den_n))
        for (_, store, n, _, _, _), (acc, m, den) in zip(new, merged):
            if branch == 2:
                store(o_ref, n, acc / den)
            else:
                for ref, val in zip(state, (acc, m, den)):
                    store(ref, n, val)

    per_branch = n_groups[0]
    assert n_groups == (per_branch,) * 3 and per_branch % 2 == 0 and per_branch > 2
    total = 3 * per_branch

    def step(branch, t, parity):
        def at(delta):
            if is_static(t):
                g = branch * per_branch + t - delta
                return None if not 0 <= g < total else (g // per_branch, g % per_branch)
            return branch, t - delta

        vg, sg, cg = at(2), at(1), at(0)
        if vg is not None:
            stage_values(vg[0], vg[1], parity)
        if sg is not None:
            stage_softmax(1 - parity)
        if cg is not None:
            stage_scores(cg[0], cg[1], parity)

    def two_steps(u, carry, branch):
        step(branch, 2 * u, 0)
        step(branch, 2 * u + 1, 1)
        return carry

    for branch in range(3):
        step(branch, 0, 0)
        step(branch, 1, 1)
        lax.fori_loop(1, per_branch // 2, functools.partial(two_steps, branch=branch), 0)
    step(3, 0, 0)
    step(3, 1, 1)


def _ret_kernel(q_ref, k_ref, v_ref, g_ref, dmat_ref, qdec_ref, kdec_ref, cdec_ref, o_ref, state_s):
    C = RET_CHUNK
    G = RET_CHUNKS_PER_STEP
    lane = lax.broadcasted_iota(jnp.int32, (C, LANES), 1)
    head0 = lane < RET_KEY_DIM
    head_masks = (head0, jnp.logical_not(head0))
    bf16 = jnp.bfloat16
    state_s[...] = jnp.zeros_like(state_s)

    def group(i, carry):
        offs = [pl.multiple_of((i * G + j) * C, C) for j in range(G)]
        qh, intra, kvs = [], [], []
        for off in offs:
            q = q_ref[pl.ds(off, C), :]
            k = k_ref[pl.ds(off, C), :]
            v = v_ref[pl.ds(off, C), :]
            qh.append([jnp.where(head_masks[h], q, jnp.zeros_like(q)) for h in range(2)])
            o = []
            for h in range(2):
                s = lax.dot_general(qh[-1][h], k, (((1,), (1,)), ((), ())),
                                    preferred_element_type=jnp.float32) * dmat_ref[0, h]
                vs = slice(h * RET_VALUE_DIM, (h + 1) * RET_VALUE_DIM)
                o.append(jnp.dot(s.astype(bf16), v[:, vs], preferred_element_type=jnp.float32))
            intra.append(o)
            kd = (k.astype(jnp.float32) * kdec_ref[0]).astype(bf16)
            kvs.append(lax.dot_general(kd, v, (((0,), (0,)), ((), ())), preferred_element_type=jnp.float32))
        state = state_s[...]
        states = []
        for kv in kvs:
            states.append(state.astype(bf16))
            state = state * cdec_ref[0] + kv
        state_s[...] = state
        for j, off in enumerate(offs):
            outs = []
            for h in range(2):
                vs = slice(h * RET_VALUE_DIM, (h + 1) * RET_VALUE_DIM)
                inter = jnp.dot(qh[j][h], states[j][:, vs], preferred_element_type=jnp.float32)
                o = intra[j][h] + inter * qdec_ref[0, :, vs]
                mu = jnp.mean(o, axis=-1, keepdims=True)
                d = o - mu
                var = jnp.mean(d * d, axis=-1, keepdims=True)
                outs.append(d * lax.rsqrt(var + GN_EPS) * g_ref[pl.ds(off, C), vs].astype(jnp.float32))
            o_ref[pl.ds(off, C), :] = jnp.concatenate(outs, axis=1).astype(o_ref.dtype)
        return carry

    lax.fori_loop(0, q_ref.shape[0] // C // G, group, 0)


def _out_kernel(x_ref, a_ref, ga_ref, r_ref, w_ref, gain_ref, o_ref, perm_s):
    rows = x_ref.shape[1]
    for hp in range(ATTN_PAIRS):
        for c in range(CLASSES):
            perm_s[hp, pl.ds(c, rows // CLASSES, stride=CLASSES), :] = a_ref[0, hp, c]
    attn = jnp.concatenate([perm_s[hp] for hp in range(ATTN_PAIRS)], axis=1)
    gated = (attn * ga_ref[0].astype(jnp.float32)).astype(jnp.bfloat16)
    mixed = jnp.concatenate([gated, r_ref[0]], axis=1)
    y = x_ref[0] + jnp.dot(mixed, w_ref[...], preferred_element_type=jnp.float32)
    ms = jnp.mean(y * y, axis=-1, keepdims=True)
    o_ref[0] = y * lax.rsqrt(ms + NORM_EPS) * gain_ref[...]


def _rotation_tables(seq):
    pos = jnp.arange(seq).astype(jnp.float32)
    half = ATTN_HEAD_DIM // 2
    inv = ROPE_THETA ** (-jnp.arange(half, dtype=jnp.float32) / half)
    ang = pos[:, None] * inv[None, :]
    cos, sin = jnp.cos(ang), jnp.sin(ang)
    ca = jnp.tile(jnp.concatenate([cos, cos], axis=1), (1, LANES // ATTN_HEAD_DIM))
    sa = jnp.tile(jnp.concatenate([-sin, sin], axis=1), (1, LANES // ATTN_HEAD_DIM))
    half = RET_KEY_DIM // 2
    inv = 1.0 / (ROPE_THETA ** jnp.linspace(0.0, 1.0, half, dtype=jnp.float32))
    ang = pos[:, None] * inv[None, :]
    cos, sin = jnp.cos(ang), jnp.sin(ang)
    cr = jnp.tile(jnp.repeat(cos, 2, axis=1), (1, LANES // RET_KEY_DIM))
    sr = jnp.tile(jnp.stack([-sin, sin], axis=-1).reshape(seq, RET_KEY_DIM), (1, LANES // RET_KEY_DIM))
    return ca, sa, cr, sr


def _attention_bias():
    i = np.arange(BLOCK)
    sz = BLOCK // CLASSES
    orders = [
        CLASSES * (i % sz) + i // sz,
        i,
        i,
    ]
    out = np.zeros((3, 2, BLOCK, 2 * BLOCK), np.float32)
    for b, u in enumerate(orders):
        prev_ok = u[None, :] >= u[:, None]
        cur_ok = u[None, :] <= u[:, None]
        out[b, 0] = np.where(np.concatenate([prev_ok, cur_ok], axis=1), 0.0, NEG_BIG)
        out[b, 1] = np.where(np.concatenate([np.zeros_like(prev_ok), cur_ok], axis=1), 0.0, NEG_BIG)
    return jnp.asarray(out)


def _retention_tables():
    C = RET_CHUNK
    log_gamma = jnp.log1p(-(2.0 ** (-5.0 - jnp.arange(RET_HEADS, dtype=jnp.float32))))
    cpos = jnp.arange(C, dtype=jnp.float32)
    diff = cpos[:, None] - cpos[None, :]
    dmat = jnp.where(diff[None] >= 0, jnp.exp(jnp.maximum(diff, 0.0)[None] * log_gamma[:, None, None]), 0.0)
    k_dec = jnp.exp((C - 1 - cpos)[None, :] * log_gamma[:, None])
    q_dec = jnp.exp((cpos + 1)[None, :] * log_gamma[:, None])
    chunk_decay = jnp.exp(C * log_gamma)
    pairs = RET_HEADS // 2
    dmat = dmat.reshape(pairs, 2, C, C)
    qdec = jnp.repeat(q_dec.reshape(pairs, 2, C).transpose(0, 2, 1), RET_VALUE_DIM, axis=2)
    kdec = jnp.repeat(k_dec.reshape(pairs, 2, C).transpose(0, 2, 1), RET_KEY_DIM, axis=2)
    row_head = jnp.arange(2 * RET_KEY_DIM) // RET_KEY_DIM
    cdec = jnp.broadcast_to(chunk_decay.reshape(pairs, 2)[:, row_head][:, :, None],
                            (pairs, 2 * RET_KEY_DIM, 2 * RET_VALUE_DIM))
    return dmat, qdec, kdec, cdec


def _mixer_layer(x, norm_gain, w_in, ret_gn_gain, w_out, out_gain):
    B, S, D = x.shape
    L = S // CLASSES
    TR = ROW_TILE
    G = BLOCKS_PER_STEP
    f32, bf16 = jnp.float32, jnp.bfloat16
    params = functools.partial(pltpu.CompilerParams, vmem_limit_bytes=VMEM_LIMIT_BYTES)
    two_arb = ("arbitrary", "arbitrary")

    ca, sa, cr, sr = _rotation_tables(S)
    tab_spec = pl.BlockSpec((TR, LANES), lambda m, b: (m, 0))
    cm_shape = (B, ATTN_PAIRS, CLASSES, L, LANES)
    cm_spec = pl.BlockSpec((1, ATTN_PAIRS, CLASSES, TR // CLASSES, LANES), lambda m, b: (b, 0, 0, m, 0))

    def nat_spec(width):
        return pl.BlockSpec((1, TR, width), lambda m, b: (b, m, 0))

    def whole(shape):
        return pl.BlockSpec(shape, lambda *_: (0,) * len(shape))

    qa, ka, va, ga, qr, kr, vr, gr = pl.pallas_call(
        _proj_kernel,
        grid=(S // TR, B),
        in_specs=[nat_spec(D), whole((1, D)), whole(w_in.shape), whole((1, RET_V_WIDTH)),
                  tab_spec, tab_spec, tab_spec, tab_spec],
        out_specs=[cm_spec, cm_spec, cm_spec, nat_spec(ATTN_WIDTH),
                   nat_spec(RET_QK_WIDTH), nat_spec(RET_QK_WIDTH), nat_spec(RET_V_WIDTH), nat_spec(RET_V_WIDTH)],
        out_shape=[
            jax.ShapeDtypeStruct(cm_shape, f32), jax.ShapeDtypeStruct(cm_shape, f32),
            jax.ShapeDtypeStruct(cm_shape, f32), jax.ShapeDtypeStruct((B, S, ATTN_WIDTH), bf16),
            jax.ShapeDtypeStruct((B, S, RET_QK_WIDTH), bf16), jax.ShapeDtypeStruct((B, S, RET_QK_WIDTH), bf16),
            jax.ShapeDtypeStruct((B, S, RET_V_WIDTH), bf16), jax.ShapeDtypeStruct((B, S, RET_V_WIDTH), bf16),
        ],
        scratch_shapes=[pltpu.VMEM((3 * ATTN_PAIRS, TR, LANES), f32)],
        compiler_params=params(dimension_semantics=two_arb),
        name="proj_rope",
    )(x, norm_gain.reshape(1, D), w_in.astype(bf16), ret_gn_gain.reshape(1, RET_V_WIDTH), ca, sa, cr, sr)

    seq_spec = pl.BlockSpec((None, None, CLASSES, L, LANES), lambda b, hp: (b, hp, 0, 0, 0))
    attn = pl.pallas_call(
        _attn_kernel,
        grid=(B, ATTN_PAIRS),
        in_specs=[seq_spec, seq_spec, seq_spec,
                  whole((3, 2, BLOCK, 2 * BLOCK)), whole((2 * BLOCK, LANES))],
        out_specs=seq_spec,
        out_shape=jax.ShapeDtypeStruct(cm_shape, f32),
        scratch_shapes=[pltpu.VMEM((CLASSES, L, LANES), f32)] * 3
        + [pltpu.VMEM((2, G, 2, BLOCK, 2 * BLOCK), f32), pltpu.VMEM((2, G, 2, BLOCK, 2 * BLOCK), bf16),
           pltpu.VMEM((2, G, 2, BLOCK, LANES), f32)],
        compiler_params=params(dimension_semantics=two_arb),
        name="dilated_attention",
    )(qa, ka, va, _attention_bias(), jnp.ones((2 * BLOCK, LANES), bf16))

    dmat, qdec, kdec, cdec = _retention_tables()
    pairs_r = RET_HEADS // 2
    C = RET_CHUNK
    assert S % (C * RET_CHUNKS_PER_STEP) == 0
    qk_spec = pl.BlockSpec((None, S, 2 * RET_KEY_DIM), lambda b, hp: (b, 0, hp))
    v_spec = pl.BlockSpec((None, S, 2 * RET_VALUE_DIM), lambda b, hp: (b, 0, hp))
    ret = pl.pallas_call(
        _ret_kernel,
        grid=(B, pairs_r),
        in_specs=[qk_spec, qk_spec, v_spec, v_spec,
                  pl.BlockSpec((1, 2, C, C), lambda b, hp: (hp, 0, 0, 0)),
                  pl.BlockSpec((1, C, 2 * RET_VALUE_DIM), lambda b, hp: (hp, 0, 0)),
                  pl.BlockSpec((1, C, 2 * RET_KEY_DIM), lambda b, hp: (hp, 0, 0)),
                  pl.BlockSpec((1, 2 * RET_KEY_DIM, 2 * RET_VALUE_DIM), lambda b, hp: (hp, 0, 0))],
        out_specs=v_spec,
        out_shape=jax.ShapeDtypeStruct((B, S, RET_V_WIDTH), bf16),
        scratch_shapes=[pltpu.VMEM((2 * RET_KEY_DIM, 2 * RET_VALUE_DIM), f32)],
        compiler_params=params(dimension_semantics=two_arb),
        name="retention",
    )(qr, kr, vr, gr, dmat, qdec, kdec, cdec)

    return pl.pallas_call(
        _out_kernel,
        grid=(S // TR, B),
        in_specs=[nat_spec(D), cm_spec, nat_spec(ATTN_WIDTH), nat_spec(RET_V_WIDTH),
                  whole(w_out.shape), whole((1, D))],
        out_specs=nat_spec(D),
        out_shape=jax.ShapeDtypeStruct((B, S, D), f32),
        scratch_shapes=[pltpu.VMEM((ATTN_PAIRS, TR, LANES), f32)],
        compiler_params=params(dimension_semantics=two_arb),
        name="out_proj_norm",
    )(x, attn, ga, ret, w_out.astype(bf16), out_gain.reshape(1, D))


def kernel(x, norm_gain, w_in, ret_gn_gain, w_out, final_gain):
    assert norm_gain.shape[0] == 1, "the fused output projection + final norm assumes a single layer"
    assert x.shape[1] % (MAX_DILATION * BLOCK) == 0
    return _mixer_layer(x, norm_gain[0], w_in[0], ret_gn_gain[0], w_out[0], final_gain)
```

```python
import functools
import math

import numpy as np
import jax
import jax.numpy as jnp
from jax import lax
from jax.experimental import pallas as pl
from jax.experimental.pallas import tpu as pltpu

LANES = 128

ATTN_HEADS = 8
ATTN_HEAD_DIM = 64
ATTN_WIDTH = ATTN_HEADS * ATTN_HEAD_DIM
ATTN_PAIRS = ATTN_WIDTH // LANES
RET_HEADS = 4
RET_KEY_DIM = 64
RET_VALUE_DIM = 128
RET_QK_WIDTH = RET_HEADS * RET_KEY_DIM
RET_V_WIDTH = RET_HEADS * RET_VALUE_DIM
RET_CHUNK = 128
ROPE_THETA = 10000.0
NORM_EPS = 1e-6
GN_EPS = 1e-5

CLASSES = 4
BLOCK = 128
MAX_DILATION = 16
QUADS = MAX_DILATION // CLASSES
NEG_BIG = -1e30

VMEM_LIMIT_BYTES = 48 * 1024 * 1024
ROW_TILE = 512
BLOCKS_PER_STEP = 4
RET_CHUNKS_PER_STEP = 8


def _silu(g):
    return g / (1.0 + jnp.exp(-g))


def _proj_kernel(x_ref, gain_ref, w_ref, gn_gain_ref, ca_ref, sa_ref, cr_ref, sr_ref,
                 qc_ref, kc_ref, vc_ref, q16_ref, k16_ref, v16_ref, ga_ref, qr_ref, kr_ref, vr_ref, gr_ref,
                 perm_s, perm2_s):
    rows = x_ref.shape[1]
    lane = lax.broadcasted_iota(jnp.int32, (rows, LANES), 1)
    first_half = (lane % ATTN_HEAD_DIM) < (ATTN_HEAD_DIM // 2)
    even = (lane % 2) == 0
    bf16 = jnp.bfloat16

    def rope_half(slab, cos, sin):
        fwd = pltpu.roll(slab, LANES - ATTN_HEAD_DIM // 2, 1)
        bwd = pltpu.roll(slab, ATTN_HEAD_DIM // 2, 1)
        return slab * cos + jnp.where(first_half, fwd, bwd) * sin

    def rope_pairs(slab, cos, sin):
        fwd = pltpu.roll(slab, LANES - 1, 1)
        bwd = pltpu.roll(slab, 1, 1)
        return slab * cos + jnp.where(even, fwd, bwd) * sin

    def proj(h, lo, width):
        return jnp.dot(h, w_ref[:, lo:lo + width], preferred_element_type=jnp.float32)

    def store_permuted(cm_ref, r16_ref, hp, slot, slab):
        perm_s[slot] = slab
        for c in range(CLASSES):
            cls = perm_s[slot, pl.ds(c, rows // CLASSES, stride=CLASSES), :]
            cm_ref[0, hp, c] = cls.astype(bf16)
            perm2_s[slot, c] = cls
            for a in range(QUADS):
                r16_ref[0, hp, QUADS * a + c] = perm2_s[
                    slot, c, pl.ds(a, rows // MAX_DILATION, stride=QUADS), :].astype(bf16)

    x = x_ref[0]
    ms = jnp.mean(x * x, axis=-1, keepdims=True)
    h = (x * lax.rsqrt(ms + NORM_EPS) * gain_ref[...]).astype(bf16)
    ca, sa = ca_ref[...], sa_ref[...]
    cr, sr = cr_ref[...], sr_ref[...]

    lo = 0
    pq = proj(h, lo, ATTN_WIDTH); lo += ATTN_WIDTH
    pk = proj(h, lo, ATTN_WIDTH); lo += ATTN_WIDTH
    pv = proj(h, lo, ATTN_WIDTH); lo += ATTN_WIDTH
    pg = proj(h, lo, ATTN_WIDTH); lo += ATTN_WIDTH
    q_scale = math.log2(math.e) * ATTN_HEAD_DIM ** -0.5
    for hp in range(ATTN_PAIRS):
        ls = slice(hp * LANES, (hp + 1) * LANES)
        store_permuted(qc_ref, q16_ref, hp, hp, rope_half(pq[:, ls], ca, sa) * q_scale)
        store_permuted(kc_ref, k16_ref, hp, ATTN_PAIRS + hp, rope_half(pk[:, ls], ca, sa))
        store_permuted(vc_ref, v16_ref, hp, 2 * ATTN_PAIRS + hp, pv[:, ls])
    ga_ref[0] = _silu(pg).astype(bf16)

    pq = proj(h, lo, RET_QK_WIDTH); lo += RET_QK_WIDTH
    pk = proj(h, lo, RET_QK_WIDTH); lo += RET_QK_WIDTH
    for hp in range(RET_QK_WIDTH // LANES):
        ls = slice(hp * LANES, (hp + 1) * LANES)
        qr_ref[0, :, ls] = rope_pairs(pq[:, ls], cr, sr).astype(bf16)
        kr_ref[0, :, ls] = (rope_pairs(pk[:, ls], cr, sr) * (RET_KEY_DIM ** -0.5)).astype(bf16)
    pv = proj(h, lo, RET_V_WIDTH); lo += RET_V_WIDTH
    pg = proj(h, lo, RET_V_WIDTH); lo += RET_V_WIDTH
    vr_ref[0] = pv.astype(bf16)
    gr_ref[0] = (_silu(pg) * gn_gain_ref[...]).astype(bf16)


def _attn_kernel(qc_ref, kc_ref, vc_ref, q16_ref, k16_ref, v16_ref, bias_ref, ones_ref, o_ref,
                 acc_s, m_s, den_s, s_s, p_s, mrow_s, mpair_s):
    G = BLOCKS_PER_STEP
    L = qc_ref.shape[1]
    lane = lax.broadcasted_iota(jnp.int32, (BLOCK, LANES), 1)
    head0 = lane < ATTN_HEAD_DIM
    head_masks = (head0, jnp.logical_not(head0))
    bf16 = jnp.bfloat16
    state = (acc_s, m_s, den_s)
    sub = BLOCK // CLASSES
    blocks3 = L // QUADS // BLOCK

    def twin(ref):
        return q16_ref if ref is qc_ref else k16_ref if ref is kc_ref else v16_ref

    def is_static(v):
        return isinstance(v, (int, bool))

    def aligned(n, size):
        return n * size if is_static(n) else pl.multiple_of(n * size, size)

    def segments(branch, t):
        if branch == 0:
            def load(ref, n):
                return jnp.concatenate([ref[c, pl.ds(aligned(n, sub), sub), :] for c in range(CLASSES)], axis=0)

            def store(ref, n, val):
                for c in range(CLASSES):
                    ref[c, pl.ds(aligned(n, sub), sub), :] = val[c * sub:(c + 1) * sub, :]

            n0 = t * G
            prev = max(n0 - 1, 0) if is_static(n0) else jnp.maximum(n0 - 1, 0)
            return [(load, load, store, [n0 + g for g in range(G)], prev, n0 == 0)]
        if branch == 1:
            per_class = L // BLOCK // G
            c = t // per_class
            n0 = (t % per_class) * G

            def load(ref, n):
                return ref[c, pl.ds(aligned(n, BLOCK), BLOCK), :]

            def store(ref, n, val):
                ref[c, pl.ds(aligned(n, BLOCK), BLOCK), :] = val

            prev = max(n0 - 1, 0) if is_static(n0) else jnp.maximum(n0 - 1, 0)
            return [(load, load, store, [n0 + g for g in range(G)], prev, n0 == 0)]
        segs = []
        for j in range(G // blocks3):
            r = t * (G // blocks3) + j
            c = r % CLASSES
            a = r // CLASSES

            def load_in(ref, n, r=r):
                return twin(ref)[r, pl.ds(n * BLOCK, BLOCK), :]

            def load_state(ref, n, c=c, a=a):
                return ref[c, pl.ds(n * BLOCK * QUADS + a, BLOCK, stride=QUADS), :]

            def store(ref, n, val, r=r, c=c, a=a):
                if ref is o_ref:
                    o_ref[r, pl.ds(n * BLOCK, BLOCK), :] = val.astype(o_ref.dtype)
                else:
                    ref[c, pl.ds(n * BLOCK * QUADS + a, BLOCK, stride=QUADS), :] = val

            segs.append((load_in, load_state, store, list(range(blocks3)), 0, True))
        return segs

    n_groups = (CLASSES * L // BLOCK // G, CLASSES * L // BLOCK // G, MAX_DILATION * blocks3 // G)

    def stage_scores(branch, t):
        i = 0
        for load_in, _, _, blocks, prev, first in segments(branch, t):
            kb = [load_in(kc_ref, nb) for nb in [prev] + blocks]
            for g, n in enumerate(blocks):
                if g > 0:
                    bias = bias_ref[branch, 0]
                elif is_static(first):
                    bias = bias_ref[branch, 1 if first else 0]
                else:
                    bias = bias_ref[branch, jnp.where(first, 1, 0)]
                keys = jnp.concatenate([kb[g], kb[g + 1]], axis=0)
                q = load_in(qc_ref, n)
                for h in range(2):
                    qh = q * ones_ref[h]
                    s = lax.dot_general(qh, keys, (((1,), (1,)), ((), ())),
                                        preferred_element_type=jnp.float32) + bias
                    s_s[i, h] = s
                    mrow_s[i, h] = jnp.broadcast_to(jnp.max(s, axis=1, keepdims=True), (BLOCK, LANES))
                i += 1

    def stage_softmax():
        for i in range(G):
            ms = []
            for h in range(2):
                m = mrow_s[i, h]
                p_s[i, h] = jnp.exp2(s_s[i, h] - jnp.concatenate([m, m], axis=1)).astype(bf16)
                ms.append(m)
            mpair_s[i] = jnp.where(head0, ms[0], ms[1])

    def stage_values(branch, t):
        i = 0
        for load_in, load_state, store, blocks, prev, _ in segments(branch, t):
            wb = []
            for nb in [prev] + blocks:
                v = load_in(vc_ref, nb)
                wb.append([jnp.concatenate([v * ones_ref[h], ones_ref[h]], axis=1)
                           for h in range(2)])
            for g, n in enumerate(blocks):
                weights = jnp.concatenate([wb[g][0], wb[g + 1][0], wb[g][1], wb[g + 1][1]], axis=0)
                probs = jnp.concatenate([p_s[i, 0], p_s[i, 1]], axis=1)
                r = jnp.dot(probs, weights, preferred_element_type=jnp.float32)
                acc, den = r[:, :LANES], r[:, LANES:]
                m = mpair_s[i]
                i += 1
                if branch > 0:
                    acc_o, m_o, den_o = (load_state(ref, n) for ref in state)
                    m_n, m = m, jnp.maximum(m_o, m)
                    w_o = jnp.exp2(m_o - m)
                    w_n = jnp.exp2(m_n - m)
                    acc = w_o * acc_o + w_n * acc
                    den = w_o * den_o + w_n * den
                if branch == 2:
                    store(o_ref, n, acc / den)
                else:
                    for ref, val in zip(state, (acc, m, den)):
                        store(ref, n, val)

    per_branch = n_groups[0]
    assert n_groups == (per_branch,) * 3
    total = 3 * per_branch
    for tau in range(total + 2):
        if 0 <= tau - 2 < total:
            stage_values((tau - 2) // per_branch, (tau - 2) % per_branch)
        if 0 <= tau - 1 < total:
            stage_softmax()
        if tau < total:
            stage_scores(tau // per_branch, tau % per_branch)


def _ret_kernel(q_ref, k_ref, v_ref, g_ref, dmat_ref, qdec_ref, kdec_ref, cdec_ref, o_ref, state_s):
    C = RET_CHUNK
    G = RET_CHUNKS_PER_STEP
    lane = lax.broadcasted_iota(jnp.int32, (C, LANES), 1)
    head0 = lane < RET_KEY_DIM
    head_masks = (head0, jnp.logical_not(head0))
    bf16 = jnp.bfloat16
    state_s[...] = jnp.zeros_like(state_s)

    def group(i, carry):
        offs = [pl.multiple_of((i * G + j) * C, C) for j in range(G)]
        qh, intra, kvs = [], [], []
        for off in offs:
            q = q_ref[pl.ds(off, C), :]
            k = k_ref[pl.ds(off, C), :]
            v = v_ref[pl.ds(off, C), :]
            qh.append([jnp.where(head_masks[h], q, jnp.zeros_like(q)) for h in range(2)])
            o = []
            for h in range(2):
                s = lax.dot_general(qh[-1][h], k, (((1,), (1,)), ((), ())),
                                    preferred_element_type=jnp.float32) * dmat_ref[0, h]
                vs = slice(h * RET_VALUE_DIM, (h + 1) * RET_VALUE_DIM)
                o.append(jnp.dot(s.astype(bf16), v[:, vs], preferred_element_type=jnp.float32))
            intra.append(o)
            kd = (k.astype(jnp.float32) * kdec_ref[0]).astype(bf16)
            kvs.append(lax.dot_general(kd, v, (((0,), (0,)), ((), ())), preferred_element_type=jnp.float32))
        state = state_s[...]
        states = []
        for kv in kvs:
            states.append(state.astype(bf16))
            state = state * cdec_ref[0] + kv
        state_s[...] = state
        for j, off in enumerate(offs):
            outs = []
            for h in range(2):
                vs = slice(h * RET_VALUE_DIM, (h + 1) * RET_VALUE_DIM)
                inter = jnp.dot(qh[j][h], states[j][:, vs], preferred_element_type=jnp.float32)
                o = intra[j][h] + inter * qdec_ref[0, :, vs]
                mu = jnp.mean(o, axis=-1, keepdims=True)
                d = o - mu
                var = jnp.mean(d * d, axis=-1, keepdims=True)
                outs.append(d * lax.rsqrt(var + GN_EPS) * g_ref[pl.ds(off, C), vs].astype(jnp.float32))
            o_ref[pl.ds(off, C), :] = jnp.concatenate(outs, axis=1).astype(o_ref.dtype)
        return carry

    lax.fori_loop(0, q_ref.shape[0] // C // G, group, 0)


def _out_kernel(x_ref, a_ref, ga_ref, r_ref, w_ref, gain_ref, o_ref, perm_s, perm2_s):
    rows = x_ref.shape[1]
    for hp in range(ATTN_PAIRS):
        for c in range(CLASSES):
            for a in range(QUADS):
                perm2_s[hp, c, pl.ds(a, rows // MAX_DILATION, stride=QUADS), :] = (
                    a_ref[0, hp, QUADS * a + c].astype(jnp.float32))
            perm_s[hp, pl.ds(c, rows // CLASSES, stride=CLASSES), :] = perm2_s[hp, c]
    attn = jnp.concatenate([perm_s[hp] for hp in range(ATTN_PAIRS)], axis=1)
    gated = (attn * ga_ref[0].astype(jnp.float32)).astype(jnp.bfloat16)
    mixed = jnp.concatenate([gated, r_ref[0]], axis=1)
    y = x_ref[0] + jnp.dot(mixed, w_ref[...], preferred_element_type=jnp.float32)
    ms = jnp.mean(y * y, axis=-1, keepdims=True)
    o_ref[0] = y * lax.rsqrt(ms + NORM_EPS) * gain_ref[...]


def _rotation_tables(seq):
    pos = jnp.arange(seq).astype(jnp.float32)
    half = ATTN_HEAD_DIM // 2
    inv = ROPE_THETA ** (-jnp.arange(half, dtype=jnp.float32) / half)
    ang = pos[:, None] * inv[None, :]
    cos, sin = jnp.cos(ang), jnp.sin(ang)
    ca = jnp.tile(jnp.concatenate([cos, cos], axis=1), (1, LANES // ATTN_HEAD_DIM))
    sa = jnp.tile(jnp.concatenate([-sin, sin], axis=1), (1, LANES // ATTN_HEAD_DIM))
    half = RET_KEY_DIM // 2
    inv = 1.0 / (ROPE_THETA ** jnp.linspace(0.0, 1.0, half, dtype=jnp.float32))
    ang = pos[:, None] * inv[None, :]
    cos, sin = jnp.cos(ang), jnp.sin(ang)
    cr = jnp.tile(jnp.repeat(cos, 2, axis=1), (1, LANES // RET_KEY_DIM))
    sr = jnp.tile(jnp.stack([-sin, sin], axis=-1).reshape(seq, RET_KEY_DIM), (1, LANES // RET_KEY_DIM))
    return ca, sa, cr, sr


def _attention_tables():
    i = np.arange(BLOCK)
    sz = BLOCK // CLASSES
    orders = [
        CLASSES * (i % sz) + i // sz,
        i,
        i,
    ]
    bias = np.zeros((3, 2, BLOCK, 2 * BLOCK), np.float32)
    for b, u in enumerate(orders):
        prev_ok = u[None, :] >= u[:, None]
        cur_ok = u[None, :] <= u[:, None]
        bias[b, 0] = np.where(np.concatenate([prev_ok, cur_ok], axis=1), 0.0, NEG_BIG)
        bias[b, 1] = np.where(np.concatenate([np.zeros_like(prev_ok), cur_ok], axis=1), 0.0, NEG_BIG)
    lane_head = np.arange(LANES) // ATTN_HEAD_DIM
    ones = np.broadcast_to((lane_head[None, :] == np.arange(2)[:, None])[:, None, :], (2, BLOCK, LANES))
    return jnp.asarray(bias), jnp.asarray(ones, dtype=jnp.bfloat16)


def _retention_tables():
    C = RET_CHUNK
    log_gamma = jnp.log1p(-(2.0 ** (-5.0 - jnp.arange(RET_HEADS, dtype=jnp.float32))))
    cpos = jnp.arange(C, dtype=jnp.float32)
    diff = cpos[:, None] - cpos[None, :]
    dmat = jnp.where(diff[None] >= 0, jnp.exp(jnp.maximum(diff, 0.0)[None] * log_gamma[:, None, None]), 0.0)
    k_dec = jnp.exp((C - 1 - cpos)[None, :] * log_gamma[:, None])
    q_dec = jnp.exp((cpos + 1)[None, :] * log_gamma[:, None])
    chunk_decay = jnp.exp(C * log_gamma)
    pairs = RET_HEADS // 2
    dmat = dmat.reshape(pairs, 2, C, C)
    qdec = jnp.repeat(q_dec.reshape(pairs, 2, C).transpose(0, 2, 1), RET_VALUE_DIM, axis=2)
    kdec = jnp.repeat(k_dec.reshape(pairs, 2, C).transpose(0, 2, 1), RET_KEY_DIM, axis=2)
    row_head = jnp.arange(2 * RET_KEY_DIM) // RET_KEY_DIM
    cdec = jnp.broadcast_to(chunk_decay.reshape(pairs, 2)[:, row_head][:, :, None],
                            (pairs, 2 * RET_KEY_DIM, 2 * RET_VALUE_DIM))
    return dmat, qdec, kdec, cdec


def _mixer_layer(x, norm_gain, w_in, ret_gn_gain, w_out, out_gain):
    B, S, D = x.shape
    L = S // CLASSES
    TR = ROW_TILE
    G = BLOCKS_PER_STEP
    f32, bf16 = jnp.float32, jnp.bfloat16
    params = functools.partial(pltpu.CompilerParams, vmem_limit_bytes=VMEM_LIMIT_BYTES)
    two_arb = ("arbitrary", "arbitrary")

    ca, sa, cr, sr = _rotation_tables(S)
    tab_spec = pl.BlockSpec((TR, LANES), lambda m, b: (m, 0))
    cm_shape = (B, ATTN_PAIRS, CLASSES, L, LANES)
    cm_spec = pl.BlockSpec((1, ATTN_PAIRS, CLASSES, TR // CLASSES, LANES), lambda m, b: (b, 0, 0, m, 0))
    r16_shape = (B, ATTN_PAIRS, MAX_DILATION, S // MAX_DILATION, LANES)
    r16_spec = pl.BlockSpec((1, ATTN_PAIRS, MAX_DILATION, TR // MAX_DILATION, LANES), lambda m, b: (b, 0, 0, m, 0))

    def nat_spec(width):
        return pl.BlockSpec((1, TR, width), lambda m, b: (b, m, 0))

    def whole(shape):
        return pl.BlockSpec(shape, lambda *_: (0,) * len(shape))

    qc, kc, vc, q16, k16, v16, ga, qr, kr, vr, gr = pl.pallas_call(
        _proj_kernel,
        grid=(S // TR, B),
        in_specs=[nat_spec(D), whole((1, D)), whole(w_in.shape), whole((1, RET_V_WIDTH)),
                  tab_spec, tab_spec, tab_spec, tab_spec],
        out_specs=[cm_spec, cm_spec, cm_spec, r16_spec, r16_spec, r16_spec, nat_spec(ATTN_WIDTH),
                   nat_spec(RET_QK_WIDTH), nat_spec(RET_QK_WIDTH), nat_spec(RET_V_WIDTH), nat_spec(RET_V_WIDTH)],
        out_shape=[jax.ShapeDtypeStruct(cm_shape, bf16)] * 3 + [jax.ShapeDtypeStruct(r16_shape, bf16)] * 3 + [
            jax.ShapeDtypeStruct((B, S, ATTN_WIDTH), bf16),
            jax.ShapeDtypeStruct((B, S, RET_QK_WIDTH), bf16), jax.ShapeDtypeStruct((B, S, RET_QK_WIDTH), bf16),
            jax.ShapeDtypeStruct((B, S, RET_V_WIDTH), bf16), jax.ShapeDtypeStruct((B, S, RET_V_WIDTH), bf16),
        ],
        scratch_shapes=[pltpu.VMEM((3 * ATTN_PAIRS, TR, LANES), f32),
                        pltpu.VMEM((3 * ATTN_PAIRS, CLASSES, TR // CLASSES, LANES), f32)],
        compiler_params=params(dimension_semantics=two_arb),
        name="proj_rope",
    )(x, norm_gain.reshape(1, D), w_in.astype(bf16), ret_gn_gain.reshape(1, RET_V_WIDTH), ca, sa, cr, sr)

    cm_seq = pl.BlockSpec((None, None, CLASSES, L, LANES), lambda b, hp: (b, hp, 0, 0, 0))
    r16_seq = pl.BlockSpec((None, None, MAX_DILATION, S // MAX_DILATION, LANES), lambda b, hp: (b, hp, 0, 0, 0))
    bias, ones = _attention_tables()
    attn = pl.pallas_call(
        _attn_kernel,
        grid=(B, ATTN_PAIRS),
        in_specs=[cm_seq, cm_seq, cm_seq, r16_seq, r16_seq, r16_seq, whole(bias.shape), whole(ones.shape)],
        out_specs=r16_seq,
        out_shape=jax.ShapeDtypeStruct(r16_shape, bf16),
        scratch_shapes=[pltpu.VMEM((CLASSES, L, LANES), f32)] * 3
        + [pltpu.VMEM((G, 2, BLOCK, 2 * BLOCK), f32), pltpu.VMEM((G, 2, BLOCK, 2 * BLOCK), bf16),
           pltpu.VMEM((G, 2, BLOCK, LANES), f32), pltpu.VMEM((G, BLOCK, LANES), f32)],
        compiler_params=params(dimension_semantics=two_arb),
        name="dilated_attention",
    )(qc, kc, vc, q16, k16, v16, bias, ones)

    dmat, qdec, kdec, cdec = _retention_tables()
    pairs_r = RET_HEADS // 2
    C = RET_CHUNK
    assert S % (C * RET_CHUNKS_PER_STEP) == 0
    qk_spec = pl.BlockSpec((None, S, 2 * RET_KEY_DIM), lambda b, hp: (b, 0, hp))
    v_spec = pl.BlockSpec((None, S, 2 * RET_VALUE_DIM), lambda b, hp: (b, 0, hp))
    ret = pl.pallas_call(
        _ret_kernel,
        grid=(B, pairs_r),
        in_specs=[qk_spec, qk_spec, v_spec, v_spec,
                  pl.BlockSpec((1, 2, C, C), lambda b, hp: (hp, 0, 0, 0)),
                  pl.BlockSpec((1, C, 2 * RET_VALUE_DIM), lambda b, hp: (hp, 0, 0)),
                  pl.BlockSpec((1, C, 2 * RET_KEY_DIM), lambda b, hp: (hp, 0, 0)),
                  pl.BlockSpec((1, 2 * RET_KEY_DIM, 2 * RET_VALUE_DIM), lambda b, hp: (hp, 0, 0))],
        out_specs=v_spec,
        out_shape=jax.ShapeDtypeStruct((B, S, RET_V_WIDTH), bf16),
        scratch_shapes=[pltpu.VMEM((2 * RET_KEY_DIM, 2 * RET_VALUE_DIM), f32)],
        compiler_params=params(dimension_semantics=two_arb),
        name="retention",
    )(qr, kr, vr, gr, dmat, qdec, kdec, cdec)

    return pl.pallas_call(
        _out_kernel,
        grid=(S // TR, B),
        in_specs=[nat_spec(D), r16_spec, nat_spec(ATTN_WIDTH), nat_spec(RET_V_WIDTH),
                  whole(w_out.shape), whole((1, D))],
        out_specs=nat_spec(D),
        out_shape=jax.ShapeDtypeStruct((B, S, D), f32),
        scratch_shapes=[pltpu.VMEM((ATTN_PAIRS, TR, LANES), f32),
                        pltpu.VMEM((ATTN_PAIRS, CLASSES, TR // CLASSES, LANES), f32)],
        compiler_params=params(dimension_semantics=two_arb),
        name="out_proj_norm",
    )(x, attn, ga, ret, w_out.astype(bf16), out_gain.reshape(1, D))


def kernel(x, norm_gain, w_in, ret_gn_gain, w_out, final_gain):
    assert norm_gain.shape[0] == 1, "the fused output projection + final norm assumes a single layer"
    assert x.shape[1] % (MAX_DILATION * BLOCK) == 0
    return _mixer_layer(x, norm_gain[0], w_in[0], ret_gn_gain[0], w_out[0], final_gain)
```

```python
import functools
import math

import numpy as np
import jax
import jax.numpy as jnp
from jax import lax
from jax.experimental import pallas as pl
from jax.experimental.pallas import tpu as pltpu

LANES = 128

ATTN_HEADS = 8
ATTN_HEAD_DIM = 64
ATTN_WIDTH = ATTN_HEADS * ATTN_HEAD_DIM
ATTN_PAIRS = ATTN_WIDTH // LANES
RET_HEADS = 4
RET_KEY_DIM = 64
RET_VALUE_DIM = 128
RET_QK_WIDTH = RET_HEADS * RET_KEY_DIM
RET_V_WIDTH = RET_HEADS * RET_VALUE_DIM
RET_CHUNK = 128
ROPE_THETA = 10000.0
NORM_EPS = 1e-6
GN_EPS = 1e-5

CLASSES = 4
BLOCK = 128
MAX_DILATION = 16
QUADS = MAX_DILATION // CLASSES
NEG_BIG = -1e30

VMEM_LIMIT_BYTES = 48 * 1024 * 1024
ROW_TILE = 512
OUT_ROW_TILE = 1024
BLOCKS_PER_STEP = 4
RET_CHUNKS_PER_STEP = 8


def _silu(g):
    return g / (1.0 + jnp.exp(-g))


def _proj_kernel(x_ref, gain_ref, w_ref, gn_gain_ref, ca_ref, sa_ref, cr_ref, sr_ref,
                 qc_ref, kc_ref, vc_ref, q16_ref, k16_ref, v16_ref, ga_ref, qr_ref, kr_ref, vr_ref, gr_ref,
                 perm_s, perm2_s):
    rows = x_ref.shape[1]
    lane = lax.broadcasted_iota(jnp.int32, (rows, LANES), 1)
    first_half = (lane % ATTN_HEAD_DIM) < (ATTN_HEAD_DIM // 2)
    even = (lane % 2) == 0
    bf16 = jnp.bfloat16

    def rope_half(slab, cos, sin):
        fwd = pltpu.roll(slab, LANES - ATTN_HEAD_DIM // 2, 1)
        bwd = pltpu.roll(slab, ATTN_HEAD_DIM // 2, 1)
        return slab * cos + jnp.where(first_half, fwd, bwd) * sin

    def rope_pairs(slab, cos, sin):
        fwd = pltpu.roll(slab, LANES - 1, 1)
        bwd = pltpu.roll(slab, 1, 1)
        return slab * cos + jnp.where(even, fwd, bwd) * sin

    def proj(h, lo, width):
        return jnp.dot(h, w_ref[:, lo:lo + width], preferred_element_type=jnp.float32)

    def store_permuted(cm_ref, r16_ref, hp, slot, slab):
        perm_s[slot] = slab
        for c in range(CLASSES):
            cls = perm_s[slot, pl.ds(c, rows // CLASSES, stride=CLASSES), :]
            cm_ref[0, hp, c] = cls.astype(bf16)
            perm2_s[slot, c] = cls
            for a in range(QUADS):
                r16_ref[0, hp, QUADS * a + c] = perm2_s[
                    slot, c, pl.ds(a, rows // MAX_DILATION, stride=QUADS), :].astype(bf16)

    x = x_ref[0]
    ms = jnp.mean(x * x, axis=-1, keepdims=True)
    h = (x * lax.rsqrt(ms + NORM_EPS) * gain_ref[...]).astype(bf16)
    ca, sa = ca_ref[...], sa_ref[...]
    cr, sr = cr_ref[...], sr_ref[...]

    lo = 0
    pq = proj(h, lo, ATTN_WIDTH); lo += ATTN_WIDTH
    pk = proj(h, lo, ATTN_WIDTH); lo += ATTN_WIDTH
    pv = proj(h, lo, ATTN_WIDTH); lo += ATTN_WIDTH
    pg = proj(h, lo, ATTN_WIDTH); lo += ATTN_WIDTH
    q_scale = math.log2(math.e) * ATTN_HEAD_DIM ** -0.5
    for hp in range(ATTN_PAIRS):
        ls = slice(hp * LANES, (hp + 1) * LANES)
        store_permuted(qc_ref, q16_ref, hp, hp, rope_half(pq[:, ls], ca, sa) * q_scale)
        store_permuted(kc_ref, k16_ref, hp, ATTN_PAIRS + hp, rope_half(pk[:, ls], ca, sa))
        store_permuted(vc_ref, v16_ref, hp, 2 * ATTN_PAIRS + hp, pv[:, ls])
    ga_ref[0] = _silu(pg).astype(bf16)

    pq = proj(h, lo, RET_QK_WIDTH); lo += RET_QK_WIDTH
    pk = proj(h, lo, RET_QK_WIDTH); lo += RET_QK_WIDTH
    for hp in range(RET_QK_WIDTH // LANES):
        ls = slice(hp * LANES, (hp + 1) * LANES)
        qr_ref[0, :, ls] = rope_pairs(pq[:, ls], cr, sr).astype(bf16)
        kr_ref[0, :, ls] = (rope_pairs(pk[:, ls], cr, sr) * (RET_KEY_DIM ** -0.5)).astype(bf16)
    pv = proj(h, lo, RET_V_WIDTH); lo += RET_V_WIDTH
    pg = proj(h, lo, RET_V_WIDTH); lo += RET_V_WIDTH
    vr_ref[0] = pv.astype(bf16)
    gr_ref[0] = (_silu(pg) * gn_gain_ref[...]).astype(bf16)


def _attn_kernel(qc_ref, kc_ref, vc_ref, q16_ref, k16_ref, v16_ref, bias_ref, ones_ref, o_ref,
                 acc_s, m_s, den_s, s_s, p_s, mrow_s, mpair_s):
    G = BLOCKS_PER_STEP
    L = qc_ref.shape[1]
    lane = lax.broadcasted_iota(jnp.int32, (BLOCK, LANES), 1)
    head0 = lane < ATTN_HEAD_DIM
    head_masks = (head0, jnp.logical_not(head0))
    bf16 = jnp.bfloat16
    state = (acc_s, m_s, den_s)
    sub = BLOCK // CLASSES
    blocks3 = L // QUADS // BLOCK

    def twin(ref):
        return q16_ref if ref is qc_ref else k16_ref if ref is kc_ref else v16_ref

    def is_static(v):
        return isinstance(v, (int, bool))

    def aligned(n, size):
        return n * size if is_static(n) else pl.multiple_of(n * size, size)

    def segments(branch, t):
        if branch == 0:
            def load(ref, n):
                return jnp.concatenate([ref[c, pl.ds(aligned(n, sub), sub), :] for c in range(CLASSES)], axis=0)

            def store(ref, n, val):
                for c in range(CLASSES):
                    ref[c, pl.ds(aligned(n, sub), sub), :] = val[c * sub:(c + 1) * sub, :]

            n0 = t * G
            prev = max(n0 - 1, 0) if is_static(n0) else jnp.maximum(n0 - 1, 0)
            return [(load, load, store, [n0 + g for g in range(G)], prev, n0 == 0)]
        if branch == 1:
            per_class = L // BLOCK // G
            c = t // per_class
            n0 = (t % per_class) * G

            def load(ref, n):
                return ref[c, pl.ds(aligned(n, BLOCK), BLOCK), :]

            def store(ref, n, val):
                ref[c, pl.ds(aligned(n, BLOCK), BLOCK), :] = val

            prev = max(n0 - 1, 0) if is_static(n0) else jnp.maximum(n0 - 1, 0)
            return [(load, load, store, [n0 + g for g in range(G)], prev, n0 == 0)]
        segs = []
        for j in range(G // blocks3):
            r = t * (G // blocks3) + j
            c = r % CLASSES
            a = r // CLASSES

            def load_in(ref, n, r=r):
                return twin(ref)[r, pl.ds(n * BLOCK, BLOCK), :]

            def load_state(ref, n, c=c, a=a):
                return ref[c, pl.ds(n * BLOCK * QUADS + a, BLOCK, stride=QUADS), :]

            def store(ref, n, val, r=r, c=c, a=a):
                if ref is o_ref:
                    o_ref[r, pl.ds(n * BLOCK, BLOCK), :] = val.astype(o_ref.dtype)
                else:
                    ref[c, pl.ds(n * BLOCK * QUADS + a, BLOCK, stride=QUADS), :] = val

            segs.append((load_in, load_state, store, list(range(blocks3)), 0, True))
        return segs

    n_groups = (CLASSES * L // BLOCK // G, CLASSES * L // BLOCK // G, MAX_DILATION * blocks3 // G)

    def stage_scores(branch, t):
        i = 0
        for load_in, _, _, blocks, prev, first in segments(branch, t):
            kb = [load_in(kc_ref, nb) for nb in [prev] + blocks]
            for g, n in enumerate(blocks):
                if g > 0:
                    bias = bias_ref[branch, 0]
                elif is_static(first):
                    bias = bias_ref[branch, 1 if first else 0]
                else:
                    bias = bias_ref[branch, jnp.where(first, 1, 0)]
                keys = jnp.concatenate([kb[g], kb[g + 1]], axis=0)
                q = load_in(qc_ref, n)
                for h in range(2):
                    qh = q * ones_ref[h]
                    s = lax.dot_general(qh, keys, (((1,), (1,)), ((), ())),
                                        preferred_element_type=jnp.float32) + bias
                    s_s[i, h] = s
                    mrow_s[i, h] = jnp.broadcast_to(jnp.max(s, axis=1, keepdims=True), (BLOCK, LANES))
                i += 1

    def stage_softmax():
        for i in range(G):
            ms = []
            for h in range(2):
                m = mrow_s[i, h]
                p_s[i, h] = jnp.exp2(s_s[i, h] - jnp.concatenate([m, m], axis=1)).astype(bf16)
                ms.append(m)
            mpair_s[i] = jnp.where(head0, ms[0], ms[1])

    def stage_values(branch, t):
        i = 0
        for load_in, load_state, store, blocks, prev, _ in segments(branch, t):
            wb = []
            for nb in [prev] + blocks:
                v = load_in(vc_ref, nb)
                wb.append([jnp.concatenate([v * ones_ref[h], ones_ref[h]], axis=1)
                           for h in range(2)])
            for g, n in enumerate(blocks):
                weights = jnp.concatenate([wb[g][0], wb[g + 1][0], wb[g][1], wb[g + 1][1]], axis=0)
                probs = jnp.concatenate([p_s[i, 0], p_s[i, 1]], axis=1)
                r = jnp.dot(probs, weights, preferred_element_type=jnp.float32)
                acc, den = r[:, :LANES], r[:, LANES:]
                m = mpair_s[i]
                i += 1
                if branch > 0:
                    acc_o, m_o, den_o = (load_state(ref, n) for ref in state)
                    m_n, m = m, jnp.maximum(m_o, m)
                    w_o = jnp.exp2(m_o - m)
                    w_n = jnp.exp2(m_n - m)
                    acc = w_o * acc_o + w_n * acc
                    den = w_o * den_o + w_n * den
                if branch == 2:
                    store(o_ref, n, acc / den)
                else:
                    for ref, val in zip(state, (acc, m, den)):
                        store(ref, n, val)

    per_branch = n_groups[0]
    assert n_groups == (per_branch,) * 3
    total = 3 * per_branch
    for tau in range(total + 2):
        if 0 <= tau - 2 < total:
            stage_values((tau - 2) // per_branch, (tau - 2) % per_branch)
        if 0 <= tau - 1 < total:
            stage_softmax()
        if tau < total:
            stage_scores(tau // per_branch, tau % per_branch)


def _ret_kernel(q_ref, k_ref, v_ref, g_ref, dmat_ref, qdec_ref, kdec_ref, cdec_ref, o_ref, state_s):
    C = RET_CHUNK
    G = RET_CHUNKS_PER_STEP
    lane = lax.broadcasted_iota(jnp.int32, (C, LANES), 1)
    head0 = lane < RET_KEY_DIM
    head_masks = (head0, jnp.logical_not(head0))
    bf16 = jnp.bfloat16
    state_s[...] = jnp.zeros_like(state_s)

    def group(i, carry):
        offs = [pl.multiple_of((i * G + j) * C, C) for j in range(G)]
        qh, intra, kvs = [], [], []
        for off in offs:
            q = q_ref[pl.ds(off, C), :]
            k = k_ref[pl.ds(off, C), :]
            v = v_ref[pl.ds(off, C), :]
            qh.append([jnp.where(head_masks[h], q, jnp.zeros_like(q)) for h in range(2)])
            o = []
            for h in range(2):
                s = lax.dot_general(qh[-1][h], k, (((1,), (1,)), ((), ())),
                                    preferred_element_type=jnp.float32) * dmat_ref[0, h]
                vs = slice(h * RET_VALUE_DIM, (h + 1) * RET_VALUE_DIM)
                o.append(jnp.dot(s.astype(bf16), v[:, vs], preferred_element_type=jnp.float32))
            intra.append(o)
            kd = (k.astype(jnp.float32) * kdec_ref[0]).astype(bf16)
            kvs.append(lax.dot_general(kd, v, (((0,), (0,)), ((), ())), preferred_element_type=jnp.float32))
        state = state_s[...]
        states = []
        for kv in kvs:
            states.append(state.astype(bf16))
            state = state * cdec_ref[0] + kv
        state_s[...] = state
        for j, off in enumerate(offs):
            outs = []
            for h in range(2):
                vs = slice(h * RET_VALUE_DIM, (h + 1) * RET_VALUE_DIM)
                inter = jnp.dot(qh[j][h], states[j][:, vs], preferred_element_type=jnp.float32)
                o = intra[j][h] + inter * qdec_ref[0, :, vs]
                mu = jnp.mean(o, axis=-1, keepdims=True)
                d = o - mu
                var = jnp.mean(d * d, axis=-1, keepdims=True)
                outs.append(d * lax.rsqrt(var + GN_EPS) * g_ref[pl.ds(off, C), vs].astype(jnp.float32))
            o_ref[pl.ds(off, C), :] = jnp.concatenate(outs, axis=1).astype(o_ref.dtype)
        return carry

    lax.fori_loop(0, q_ref.shape[0] // C // G, group, 0)


def _out_kernel(x_ref, a_ref, ga_ref, r_ref, w_ref, gain_ref, o_ref, perm_s, perm2_s):
    rows = x_ref.shape[1]
    for hp in range(ATTN_PAIRS):
        for c in range(CLASSES):
            for a in range(QUADS):
                perm2_s[hp, c, pl.ds(a, rows // MAX_DILATION, stride=QUADS), :] = (
                    a_ref[0, hp, QUADS * a + c].astype(jnp.float32))
            perm_s[hp, pl.ds(c, rows // CLASSES, stride=CLASSES), :] = perm2_s[hp, c]
    attn = jnp.concatenate([perm_s[hp] for hp in range(ATTN_PAIRS)], axis=1)
    gated = (attn * ga_ref[0].astype(jnp.float32)).astype(jnp.bfloat16)
    mixed = jnp.concatenate([gated, r_ref[0]], axis=1)
    y = x_ref[0] + jnp.dot(mixed, w_ref[...], preferred_element_type=jnp.float32)
    ms = jnp.mean(y * y, axis=-1, keepdims=True)
    o_ref[0] = y * lax.rsqrt(ms + NORM_EPS) * gain_ref[...]


def _rotation_tables(seq):
    pos = jnp.arange(seq).astype(jnp.float32)
    half = ATTN_HEAD_DIM // 2
    inv = ROPE_THETA ** (-jnp.arange(half, dtype=jnp.float32) / half)
    ang = pos[:, None] * inv[None, :]
    cos, sin = jnp.cos(ang), jnp.sin(ang)
    ca = jnp.tile(jnp.concatenate([cos, cos], axis=1), (1, LANES // ATTN_HEAD_DIM))
    sa = jnp.tile(jnp.concatenate([-sin, sin], axis=1), (1, LANES // ATTN_HEAD_DIM))
    half = RET_KEY_DIM // 2
    inv = 1.0 / (ROPE_THETA ** jnp.linspace(0.0, 1.0, half, dtype=jnp.float32))
    ang = pos[:, None] * inv[None, :]
    cos, sin = jnp.cos(ang), jnp.sin(ang)
    cr = jnp.tile(jnp.repeat(cos, 2, axis=1), (1, LANES // RET_KEY_DIM))
    sr = jnp.tile(jnp.stack([-sin, sin], axis=-1).reshape(seq, RET_KEY_DIM), (1, LANES // RET_KEY_DIM))
    return ca, sa, cr, sr


def _attention_tables():
    i = np.arange(BLOCK)
    sz = BLOCK // CLASSES
    orders = [
        CLASSES * (i % sz) + i // sz,
        i,
        i,
    ]
    bias = np.zeros((3, 2, BLOCK, 2 * BLOCK), np.float32)
    for b, u in enumerate(orders):
        prev_ok = u[None, :] >= u[:, None]
        cur_ok = u[None, :] <= u[:, None]
        bias[b, 0] = np.where(np.concatenate([prev_ok, cur_ok], axis=1), 0.0, NEG_BIG)
        bias[b, 1] = np.where(np.concatenate([np.zeros_like(prev_ok), cur_ok], axis=1), 0.0, NEG_BIG)
    lane_head = np.arange(LANES) // ATTN_HEAD_DIM
    ones = np.broadcast_to((lane_head[None, :] == np.arange(2)[:, None])[:, None, :], (2, BLOCK, LANES))
    return jnp.asarray(bias), jnp.asarray(ones, dtype=jnp.bfloat16)


def _retention_tables():
    C = RET_CHUNK
    log_gamma = jnp.log1p(-(2.0 ** (-5.0 - jnp.arange(RET_HEADS, dtype=jnp.float32))))
    cpos = jnp.arange(C, dtype=jnp.float32)
    diff = cpos[:, None] - cpos[None, :]
    dmat = jnp.where(diff[None] >= 0, jnp.exp(jnp.maximum(diff, 0.0)[None] * log_gamma[:, None, None]), 0.0)
    k_dec = jnp.exp((C - 1 - cpos)[None, :] * log_gamma[:, None])
    q_dec = jnp.exp((cpos + 1)[None, :] * log_gamma[:, None])
    chunk_decay = jnp.exp(C * log_gamma)
    pairs = RET_HEADS // 2
    dmat = dmat.reshape(pairs, 2, C, C)
    qdec = jnp.repeat(q_dec.reshape(pairs, 2, C).transpose(0, 2, 1), RET_VALUE_DIM, axis=2)
    kdec = jnp.repeat(k_dec.reshape(pairs, 2, C).transpose(0, 2, 1), RET_KEY_DIM, axis=2)
    row_head = jnp.arange(2 * RET_KEY_DIM) // RET_KEY_DIM
    cdec = jnp.broadcast_to(chunk_decay.reshape(pairs, 2)[:, row_head][:, :, None],
                            (pairs, 2 * RET_KEY_DIM, 2 * RET_VALUE_DIM))
    return dmat, qdec, kdec, cdec


def _mixer_layer(x, norm_gain, w_in, ret_gn_gain, w_out, out_gain):
    B, S, D = x.shape
    L = S // CLASSES
    TR = ROW_TILE
    G = BLOCKS_PER_STEP
    f32, bf16 = jnp.float32, jnp.bfloat16
    params = functools.partial(pltpu.CompilerParams, vmem_limit_bytes=VMEM_LIMIT_BYTES)
    two_arb = ("arbitrary", "arbitrary")

    with jax.ensure_compile_time_eval():
        ca, sa, cr, sr = _rotation_tables(S)
        dmat, qdec, kdec, cdec = _retention_tables()
    tab_spec = pl.BlockSpec((TR, LANES), lambda m, b: (m, 0))
    cm_shape = (B, ATTN_PAIRS, CLASSES, L, LANES)
    cm_spec = pl.BlockSpec((1, ATTN_PAIRS, CLASSES, TR // CLASSES, LANES), lambda m, b: (b, 0, 0, m, 0))
    r16_shape = (B, ATTN_PAIRS, MAX_DILATION, S // MAX_DILATION, LANES)

    def r16_spec(tr):
        return pl.BlockSpec((1, ATTN_PAIRS, MAX_DILATION, tr // MAX_DILATION, LANES), lambda m, b: (b, 0, 0, m, 0))

    def nat_spec(width, tr=TR):
        return pl.BlockSpec((1, tr, width), lambda m, b: (b, m, 0))

    def whole(shape):
        return pl.BlockSpec(shape, lambda *_: (0,) * len(shape))

    qc, kc, vc, q16, k16, v16, ga, qr, kr, vr, gr = pl.pallas_call(
        _proj_kernel,
        grid=(S // TR, B),
        in_specs=[nat_spec(D), whole((1, D)), whole(w_in.shape), whole((1, RET_V_WIDTH)),
                  tab_spec, tab_spec, tab_spec, tab_spec],
        out_specs=[cm_spec, cm_spec, cm_spec, r16_spec(TR), r16_spec(TR), r16_spec(TR), nat_spec(ATTN_WIDTH),
                   nat_spec(RET_QK_WIDTH), nat_spec(RET_QK_WIDTH), nat_spec(RET_V_WIDTH), nat_spec(RET_V_WIDTH)],
        out_shape=[jax.ShapeDtypeStruct(cm_shape, bf16)] * 3 + [jax.ShapeDtypeStruct(r16_shape, bf16)] * 3 + [
            jax.ShapeDtypeStruct((B, S, ATTN_WIDTH), bf16),
            jax.ShapeDtypeStruct((B, S, RET_QK_WIDTH), bf16), jax.ShapeDtypeStruct((B, S, RET_QK_WIDTH), bf16),
            jax.ShapeDtypeStruct((B, S, RET_V_WIDTH), bf16), jax.ShapeDtypeStruct((B, S, RET_V_WIDTH), bf16),
        ],
        scratch_shapes=[pltpu.VMEM((3 * ATTN_PAIRS, TR, LANES), f32),
                        pltpu.VMEM((3 * ATTN_PAIRS, CLASSES, TR // CLASSES, LANES), f32)],
        compiler_params=params(dimension_semantics=two_arb),
        name="proj_rope",
    )(x, norm_gain.reshape(1, D), w_in.astype(bf16), ret_gn_gain.reshape(1, RET_V_WIDTH), ca, sa, cr, sr)

    cm_seq = pl.BlockSpec((None, None, CLASSES, L, LANES), lambda b, hp: (b, hp, 0, 0, 0))
    r16_seq = pl.BlockSpec((None, None, MAX_DILATION, S // MAX_DILATION, LANES), lambda b, hp: (b, hp, 0, 0, 0))
    bias, ones = _attention_tables()
    attn = pl.pallas_call(
        _attn_kernel,
        grid=(B, ATTN_PAIRS),
        in_specs=[cm_seq, cm_seq, cm_seq, r16_seq, r16_seq, r16_seq, whole(bias.shape), whole(ones.shape)],
        out_specs=r16_seq,
        out_shape=jax.ShapeDtypeStruct(r16_shape, bf16),
        scratch_shapes=[pltpu.VMEM((CLASSES, L, LANES), f32)] * 3
        + [pltpu.VMEM((G, 2, BLOCK, 2 * BLOCK), f32), pltpu.VMEM((G, 2, BLOCK, 2 * BLOCK), bf16),
           pltpu.VMEM((G, 2, BLOCK, LANES), f32), pltpu.VMEM((G, BLOCK, LANES), f32)],
        compiler_params=params(dimension_semantics=two_arb),
        name="dilated_attention",
    )(qc, kc, vc, q16, k16, v16, bias, ones)

    pairs_r = RET_HEADS // 2
    C = RET_CHUNK
    assert S % (C * RET_CHUNKS_PER_STEP) == 0
    qk_spec = pl.BlockSpec((None, S, 2 * RET_KEY_DIM), lambda b, hp: (b, 0, hp))
    v_spec = pl.BlockSpec((None, S, 2 * RET_VALUE_DIM), lambda b, hp: (b, 0, hp))
    ret = pl.pallas_call(
        _ret_kernel,
        grid=(B, pairs_r),
        in_specs=[qk_spec, qk_spec, v_spec, v_spec,
                  pl.BlockSpec((1, 2, C, C), lambda b, hp: (hp, 0, 0, 0)),
                  pl.BlockSpec((1, C, 2 * RET_VALUE_DIM), lambda b, hp: (hp, 0, 0)),
                  pl.BlockSpec((1, C, 2 * RET_KEY_DIM), lambda b, hp: (hp, 0, 0)),
                  pl.BlockSpec((1, 2 * RET_KEY_DIM, 2 * RET_VALUE_DIM), lambda b, hp: (hp, 0, 0))],
        out_specs=v_spec,
        out_shape=jax.ShapeDtypeStruct((B, S, RET_V_WIDTH), bf16),
        scratch_shapes=[pltpu.VMEM((2 * RET_KEY_DIM, 2 * RET_VALUE_DIM), f32)],
        compiler_params=params(dimension_semantics=two_arb),
        name="retention",
    )(qr, kr, vr, gr, dmat, qdec, kdec, cdec)

    TO = OUT_ROW_TILE
    return pl.pallas_call(
        _out_kernel,
        grid=(S // TO, B),
        in_specs=[nat_spec(D, TO), r16_spec(TO), nat_spec(ATTN_WIDTH, TO), nat_spec(RET_V_WIDTH, TO),
                  whole(w_out.shape), whole((1, D))],
        out_specs=nat_spec(D, TO),
        out_shape=jax.ShapeDtypeStruct((B, S, D), f32),
        scratch_shapes=[pltpu.VMEM((ATTN_PAIRS, TO, LANES), f32),
                        pltpu.VMEM((ATTN_PAIRS, CLASSES, TO // CLASSES, LANES), f32)],
        compiler_params=params(dimension_semantics=two_arb),
        name="out_proj_norm",
    )(x, attn, ga, ret, w_out.astype(bf16), out_gain.reshape(1, D))


def kernel(x, norm_gain, w_in, ret_gn_gain, w_out, final_gain):
    assert norm_gain.shape[0] == 1, "the fused output projection + final norm assumes a single layer"
    assert x.shape[1] % (MAX_DILATION * BLOCK) == 0
    return _mixer_layer(x, norm_gain[0], w_in[0], ret_gn_gain[0], w_out[0], final_gain)
```

```python
import functools
import math

import numpy as np
import jax
import jax.numpy as jnp
from jax import lax
from jax.experimental import pallas as pl
from jax.experimental.pallas import tpu as pltpu

LANES = 128

ATTN_HEADS = 8
ATTN_HEAD_DIM = 64
ATTN_WIDTH = ATTN_HEADS * ATTN_HEAD_DIM
ATTN_PAIRS = ATTN_WIDTH // LANES
RET_HEADS = 4
RET_KEY_DIM = 64
RET_VALUE_DIM = 128
RET_QK_WIDTH = RET_HEADS * RET_KEY_DIM
RET_V_WIDTH = RET_HEADS * RET_VALUE_DIM
RET_CHUNK = 128
ROPE_THETA = 10000.0
NORM_EPS = 1e-6
GN_EPS = 1e-5

CLASSES = 4
BLOCK = 128
MAX_DILATION = 16
QUADS = MAX_DILATION // CLASSES
NEG_BIG = -1e30

VMEM_LIMIT_BYTES = 52 * 1024 * 1024
ROW_TILE = 1024
PROJ_SUB_ROWS = 512
OUT_ROW_TILE = 1024
BLOCKS_PER_STEP = 4
RET_CHUNKS_PER_STEP = 8


def _silu(g):
    return g / (1.0 + jnp.exp(-g))


def _proj_kernel(x_ref, gain_ref, w_ref, gn_gain_ref, ca_ref, sa_ref, cr_ref, sr_ref,
                 qc_ref, kc_ref, vc_ref, q16_ref, k16_ref, v16_ref, ga_ref, qr_ref, kr_ref, vr_ref, gr_ref,
                 perm_s, perm2_s):
    rows = PROJ_SUB_ROWS
    lane = lax.broadcasted_iota(jnp.int32, (rows, LANES), 1)
    first_half = (lane % ATTN_HEAD_DIM) < (ATTN_HEAD_DIM // 2)
    even = (lane % 2) == 0
    bf16 = jnp.bfloat16

    def rope_half(slab, cos, sin):
        fwd = pltpu.roll(slab, LANES - ATTN_HEAD_DIM // 2, 1)
        bwd = pltpu.roll(slab, ATTN_HEAD_DIM // 2, 1)
        return slab * cos + jnp.where(first_half, fwd, bwd) * sin

    def rope_pairs(slab, cos, sin):
        fwd = pltpu.roll(slab, LANES - 1, 1)
        bwd = pltpu.roll(slab, 1, 1)
        return slab * cos + jnp.where(even, fwd, bwd) * sin

    def proj(h, lo, width):
        return jnp.dot(h, w_ref[:, lo:lo + width], preferred_element_type=jnp.float32)

    for sub in range(x_ref.shape[1] // rows):
        rs = pl.ds(sub * rows, rows)

        def store_permuted(cm_ref, r16_ref, hp, slot, slab):
            perm_s[slot] = slab
            for c in range(CLASSES):
                cls = perm_s[slot, pl.ds(c, rows // CLASSES, stride=CLASSES), :]
                cm_ref[0, hp, c, pl.ds(sub * rows // CLASSES, rows // CLASSES), :] = cls.astype(bf16)
                perm2_s[slot, c] = cls
                for a in range(QUADS):
                    r16_ref[0, hp, QUADS * a + c, pl.ds(sub * rows // MAX_DILATION, rows // MAX_DILATION), :] = (
                        perm2_s[slot, c, pl.ds(a, rows // MAX_DILATION, stride=QUADS), :].astype(bf16))

        x = x_ref[0, rs, :]
        ms = jnp.mean(x * x, axis=-1, keepdims=True)
        h = (x * lax.rsqrt(ms + NORM_EPS) * gain_ref[...]).astype(bf16)
        ca, sa = ca_ref[rs, :], sa_ref[rs, :]
        cr, sr = cr_ref[rs, :], sr_ref[rs, :]

        lo = 0
        pq = proj(h, lo, ATTN_WIDTH); lo += ATTN_WIDTH
        pk = proj(h, lo, ATTN_WIDTH); lo += ATTN_WIDTH
        pv = proj(h, lo, ATTN_WIDTH); lo += ATTN_WIDTH
        pg = proj(h, lo, ATTN_WIDTH); lo += ATTN_WIDTH
        q_scale = math.log2(math.e) * ATTN_HEAD_DIM ** -0.5
        for hp in range(ATTN_PAIRS):
            ls = slice(hp * LANES, (hp + 1) * LANES)
            store_permuted(qc_ref, q16_ref, hp, hp, rope_half(pq[:, ls], ca, sa) * q_scale)
            store_permuted(kc_ref, k16_ref, hp, ATTN_PAIRS + hp, rope_half(pk[:, ls], ca, sa))
            store_permuted(vc_ref, v16_ref, hp, 2 * ATTN_PAIRS + hp, pv[:, ls])
        ga_ref[0, rs, :] = _silu(pg).astype(bf16)

        pq = proj(h, lo, RET_QK_WIDTH); lo += RET_QK_WIDTH
        pk = proj(h, lo, RET_QK_WIDTH); lo += RET_QK_WIDTH
        for hp in range(RET_QK_WIDTH // LANES):
            ls = slice(hp * LANES, (hp + 1) * LANES)
            qr_ref[0, rs, ls] = rope_pairs(pq[:, ls], cr, sr).astype(bf16)
            kr_ref[0, rs, ls] = (rope_pairs(pk[:, ls], cr, sr) * (RET_KEY_DIM ** -0.5)).astype(bf16)
        pv = proj(h, lo, RET_V_WIDTH); lo += RET_V_WIDTH
        pg = proj(h, lo, RET_V_WIDTH); lo += RET_V_WIDTH
        vr_ref[0, rs, :] = pv.astype(bf16)
        gr_ref[0, rs, :] = (_silu(pg) * gn_gain_ref[...]).astype(bf16)


def _attn_kernel(qc_ref, kc_ref, vc_ref, q16_ref, k16_ref, v16_ref, bias_ref, ones_ref, o_ref,
                 acc_s, m_s, den_s, s_s, p_s, mrow_s, mpair_s):
    G = BLOCKS_PER_STEP
    L = qc_ref.shape[1]
    lane = lax.broadcasted_iota(jnp.int32, (BLOCK, LANES), 1)
    head0 = lane < ATTN_HEAD_DIM
    head_masks = (head0, jnp.logical_not(head0))
    bf16 = jnp.bfloat16
    state = (acc_s, m_s, den_s)
    sub = BLOCK // CLASSES
    blocks3 = L // QUADS // BLOCK

    def twin(ref):
        return q16_ref if ref is qc_ref else k16_ref if ref is kc_ref else v16_ref

    def is_static(v):
        return isinstance(v, (int, bool))

    def aligned(n, size):
        return n * size if is_static(n) else pl.multiple_of(n * size, size)

    def segments(branch, t):
        if branch == 0:
            def load(ref, n):
                return jnp.concatenate([ref[c, pl.ds(aligned(n, sub), sub), :] for c in range(CLASSES)], axis=0)

            def store(ref, n, val):
                for c in range(CLASSES):
                    ref[c, pl.ds(aligned(n, sub), sub), :] = val[c * sub:(c + 1) * sub, :]

            n0 = t * G
            prev = max(n0 - 1, 0) if is_static(n0) else jnp.maximum(n0 - 1, 0)
            return [(load, load, store, [n0 + g for g in range(G)], prev, n0 == 0)]
        if branch == 1:
            per_class = L // BLOCK // G
            c = t // per_class
            n0 = (t % per_class) * G

            def load(ref, n):
                return ref[c, pl.ds(aligned(n, BLOCK), BLOCK), :]

            def store(ref, n, val):
                ref[c, pl.ds(aligned(n, BLOCK), BLOCK), :] = val

            prev = max(n0 - 1, 0) if is_static(n0) else jnp.maximum(n0 - 1, 0)
            return [(load, load, store, [n0 + g for g in range(G)], prev, n0 == 0)]
        segs = []
        for j in range(G // blocks3):
            r = t * (G // blocks3) + j
            c = r % CLASSES
            a = r // CLASSES

            def load_in(ref, n, r=r):
                return twin(ref)[r, pl.ds(n * BLOCK, BLOCK), :]

            def load_state(ref, n, c=c, a=a):
                return ref[c, pl.ds(n * BLOCK * QUADS + a, BLOCK, stride=QUADS), :]

            def store(ref, n, val, r=r, c=c, a=a):
                if ref is o_ref:
                    o_ref[r, pl.ds(n * BLOCK, BLOCK), :] = val.astype(o_ref.dtype)
                else:
                    ref[c, pl.ds(n * BLOCK * QUADS + a, BLOCK, stride=QUADS), :] = val

            segs.append((load_in, load_state, store, list(range(blocks3)), 0, True))
        return segs

    n_groups = (CLASSES * L // BLOCK // G, CLASSES * L // BLOCK // G, MAX_DILATION * blocks3 // G)

    def stage_scores(branch, t):
        i = 0
        for load_in, _, _, blocks, prev, first in segments(branch, t):
            kb = [load_in(kc_ref, nb) for nb in [prev] + blocks]
            for g, n in enumerate(blocks):
                if g > 0:
                    bias = bias_ref[branch, 0]
                elif is_static(first):
                    bias = bias_ref[branch, 1 if first else 0]
                else:
                    bias = bias_ref[branch, jnp.where(first, 1, 0)]
                keys = jnp.concatenate([kb[g], kb[g + 1]], axis=0)
                q = load_in(qc_ref, n)
                for h in range(2):
                    qh = q * ones_ref[h]
                    s = lax.dot_general(qh, keys, (((1,), (1,)), ((), ())),
                                        preferred_element_type=jnp.float32) + bias
                    s_s[i, h] = s
                    mrow_s[i, h] = jnp.broadcast_to(jnp.max(s, axis=1, keepdims=True), (BLOCK, LANES))
                i += 1

    def stage_softmax():
        for i in range(G):
            ms = []
            for h in range(2):
                m = mrow_s[i, h]
                p_s[i, h] = jnp.exp2(s_s[i, h] - jnp.concatenate([m, m], axis=1)).astype(bf16)
                ms.append(m)
            mpair_s[i] = jnp.where(head0, ms[0], ms[1])

    def stage_values(branch, t):
        i = 0
        for load_in, load_state, store, blocks, prev, _ in segments(branch, t):
            wb = []
            for nb in [prev] + blocks:
                v = load_in(vc_ref, nb)
                wb.append([jnp.concatenate([v * ones_ref[h], ones_ref[h]], axis=1)
                           for h in range(2)])
            for g, n in enumerate(blocks):
                weights = jnp.concatenate([wb[g][0], wb[g + 1][0], wb[g][1], wb[g + 1][1]], axis=0)
                probs = jnp.concatenate([p_s[i, 0], p_s[i, 1]], axis=1)
                r = jnp.dot(probs, weights, preferred_element_type=jnp.float32)
                acc, den = r[:, :LANES], r[:, LANES:]
                m = mpair_s[i]
                i += 1
                if branch > 0:
                    acc_o, m_o, den_o = (load_state(ref, n) for ref in state)
                    m_n, m = m, jnp.maximum(m_o, m)
                    w_o = jnp.exp2(m_o - m)
                    w_n = jnp.exp2(m_n - m)
                    acc = w_o * acc_o + w_n * acc
                    den = w_o * den_o + w_n * den
                if branch == 2:
                    store(o_ref, n, acc / den)
                else:
                    for ref, val in zip(state, (acc, m, den)):
                        store(ref, n, val)

    per_branch = n_groups[0]
    assert n_groups == (per_branch,) * 3
    total = 3 * per_branch
    for tau in range(total + 2):
        if 0 <= tau - 2 < total:
            stage_values((tau - 2) // per_branch, (tau - 2) % per_branch)
        if 0 <= tau - 1 < total:
            stage_softmax()
        if tau < total:
            stage_scores(tau // per_branch, tau % per_branch)


def _ret_kernel(q_ref, k_ref, v_ref, g_ref, dmat_ref, qdec_ref, kdec_ref, cdec_ref, o_ref, state_s):
    C = RET_CHUNK
    G = RET_CHUNKS_PER_STEP
    lane = lax.broadcasted_iota(jnp.int32, (C, LANES), 1)
    head0 = lane < RET_KEY_DIM
    head_masks = (head0, jnp.logical_not(head0))
    bf16 = jnp.bfloat16
    state_s[...] = jnp.zeros_like(state_s)

    def group(i, carry):
        offs = [pl.multiple_of((i * G + j) * C, C) for j in range(G)]
        qh, intra, kvs = [], [], []
        for off in offs:
            q = q_ref[pl.ds(off, C), :]
            k = k_ref[pl.ds(off, C), :]
            v = v_ref[pl.ds(off, C), :]
            qh.append([jnp.where(head_masks[h], q, jnp.zeros_like(q)) for h in range(2)])
            o = []
            for h in range(2):
                s = lax.dot_general(qh[-1][h], k, (((1,), (1,)), ((), ())),
                                    preferred_element_type=jnp.float32) * dmat_ref[0, h]
                vs = slice(h * RET_VALUE_DIM, (h + 1) * RET_VALUE_DIM)
                o.append(jnp.dot(s.astype(bf16), v[:, vs], preferred_element_type=jnp.float32))
            intra.append(o)
            kd = (k.astype(jnp.float32) * kdec_ref[0]).astype(bf16)
            kvs.append(lax.dot_general(kd, v, (((0,), (0,)), ((), ())), preferred_element_type=jnp.float32))
        state = state_s[...]
        states = []
        for kv in kvs:
            states.append(state.astype(bf16))
            state = state * cdec_ref[0] + kv
        state_s[...] = state
        for j, off in enumerate(offs):
            outs = []
            for h in range(2):
                vs = slice(h * RET_VALUE_DIM, (h + 1) * RET_VALUE_DIM)
                inter = jnp.dot(qh[j][h], states[j][:, vs], preferred_element_type=jnp.float32)
                o = intra[j][h] + inter * qdec_ref[0, :, vs]
                mu = jnp.mean(o, axis=-1, keepdims=True)
                d = o - mu
                var = jnp.mean(d * d, axis=-1, keepdims=True)
                outs.append(d * lax.rsqrt(var + GN_EPS) * g_ref[pl.ds(off, C), vs].astype(jnp.float32))
            o_ref[pl.ds(off, C), :] = jnp.concatenate(outs, axis=1).astype(o_ref.dtype)
        return carry

    lax.fori_loop(0, q_ref.shape[0] // C // G, group, 0)


def _out_kernel(x_ref, a_ref, ga_ref, r_ref, w_ref, gain_ref, o_ref, perm_s, perm2_s):
    rows = x_ref.shape[1]
    for hp in range(ATTN_PAIRS):
        for c in range(CLASSES):
            for a in range(QUADS):
                perm2_s[hp, c, pl.ds(a, rows // MAX_DILATION, stride=QUADS), :] = (
                    a_ref[0, hp, QUADS * a + c].astype(jnp.float32))
            perm_s[hp, pl.ds(c, rows // CLASSES, stride=CLASSES), :] = perm2_s[hp, c]
    attn = jnp.concatenate([perm_s[hp] for hp in range(ATTN_PAIRS)], axis=1)
    gated = (attn * ga_ref[0].astype(jnp.float32)).astype(jnp.bfloat16)
    mixed = jnp.concatenate([gated, r_ref[0]], axis=1)
    y = x_ref[0] + jnp.dot(mixed, w_ref[...], preferred_element_type=jnp.float32)
    ms = jnp.mean(y * y, axis=-1, keepdims=True)
    o_ref[0] = y * lax.rsqrt(ms + NORM_EPS) * gain_ref[...]


def _rotation_tables(seq):
    pos = jnp.arange(seq).astype(jnp.float32)
    half = ATTN_HEAD_DIM // 2
    inv = ROPE_THETA ** (-jnp.arange(half, dtype=jnp.float32) / half)
    ang = pos[:, None] * inv[None, :]
    cos, sin = jnp.cos(ang), jnp.sin(ang)
    ca = jnp.tile(jnp.concatenate([cos, cos], axis=1), (1, LANES // ATTN_HEAD_DIM))
    sa = jnp.tile(jnp.concatenate([-sin, sin], axis=1), (1, LANES // ATTN_HEAD_DIM))
    half = RET_KEY_DIM // 2
    inv = 1.0 / (ROPE_THETA ** jnp.linspace(0.0, 1.0, half, dtype=jnp.float32))
    ang = pos[:, None] * inv[None, :]
    cos, sin = jnp.cos(ang), jnp.sin(ang)
    cr = jnp.tile(jnp.repeat(cos, 2, axis=1), (1, LANES // RET_KEY_DIM))
    sr = jnp.tile(jnp.stack([-sin, sin], axis=-1).reshape(seq, RET_KEY_DIM), (1, LANES // RET_KEY_DIM))
    return ca, sa, cr, sr


def _attention_tables():
    i = np.arange(BLOCK)
    sz = BLOCK // CLASSES
    orders = [
        CLASSES * (i % sz) + i // sz,
        i,
        i,
    ]
    bias = np.zeros((3, 2, BLOCK, 2 * BLOCK), np.float32)
    for b, u in enumerate(orders):
        prev_ok = u[None, :] >= u[:, None]
        cur_ok = u[None, :] <= u[:, None]
        bias[b, 0] = np.where(np.concatenate([prev_ok, cur_ok], axis=1), 0.0, NEG_BIG)
        bias[b, 1] = np.where(np.concatenate([np.zeros_like(prev_ok), cur_ok], axis=1), 0.0, NEG_BIG)
    lane_head = np.arange(LANES) // ATTN_HEAD_DIM
    ones = np.broadcast_to((lane_head[None, :] == np.arange(2)[:, None])[:, None, :], (2, BLOCK, LANES))
    return jnp.asarray(bias), jnp.asarray(ones, dtype=jnp.bfloat16)


def _retention_tables():
    C = RET_CHUNK
    log_gamma = jnp.log1p(-(2.0 ** (-5.0 - jnp.arange(RET_HEADS, dtype=jnp.float32))))
    cpos = jnp.arange(C, dtype=jnp.float32)
    diff = cpos[:, None] - cpos[None, :]
    dmat = jnp.where(diff[None] >= 0, jnp.exp(jnp.maximum(diff, 0.0)[None] * log_gamma[:, None, None]), 0.0)
    k_dec = jnp.exp((C - 1 - cpos)[None, :] * log_gamma[:, None])
    q_dec = jnp.exp((cpos + 1)[None, :] * log_gamma[:, None])
    chunk_decay = jnp.exp(C * log_gamma)
    pairs = RET_HEADS // 2
    dmat = dmat.reshape(pairs, 2, C, C)
    qdec = jnp.repeat(q_dec.reshape(pairs, 2, C).transpose(0, 2, 1), RET_VALUE_DIM, axis=2)
    kdec = jnp.repeat(k_dec.reshape(pairs, 2, C).transpose(0, 2, 1), RET_KEY_DIM, axis=2)
    row_head = jnp.arange(2 * RET_KEY_DIM) // RET_KEY_DIM
    cdec = jnp.broadcast_to(chunk_decay.reshape(pairs, 2)[:, row_head][:, :, None],
                            (pairs, 2 * RET_KEY_DIM, 2 * RET_VALUE_DIM))
    return dmat, qdec, kdec, cdec


def _mixer_layer(x, norm_gain, w_in, ret_gn_gain, w_out, out_gain):
    B, S, D = x.shape
    L = S // CLASSES
    TR = ROW_TILE
    G = BLOCKS_PER_STEP
    f32, bf16 = jnp.float32, jnp.bfloat16
    params = functools.partial(pltpu.CompilerParams, vmem_limit_bytes=VMEM_LIMIT_BYTES)
    two_arb = ("arbitrary", "arbitrary")

    with jax.ensure_compile_time_eval():
        ca, sa, cr, sr = _rotation_tables(S)
        dmat, qdec, kdec, cdec = _retention_tables()
    tab_spec = pl.BlockSpec((TR, LANES), lambda m, b: (m, 0))
    cm_shape = (B, ATTN_PAIRS, CLASSES, L, LANES)
    cm_spec = pl.BlockSpec((1, ATTN_PAIRS, CLASSES, TR // CLASSES, LANES), lambda m, b: (b, 0, 0, m, 0))
    r16_shape = (B, ATTN_PAIRS, MAX_DILATION, S // MAX_DILATION, LANES)

    def r16_spec(tr):
        return pl.BlockSpec((1, ATTN_PAIRS, MAX_DILATION, tr // MAX_DILATION, LANES), lambda m, b: (b, 0, 0, m, 0))

    def nat_spec(width, tr=TR):
        return pl.BlockSpec((1, tr, width), lambda m, b: (b, m, 0))

    def whole(shape):
        return pl.BlockSpec(shape, lambda *_: (0,) * len(shape))

    qc, kc, vc, q16, k16, v16, ga, qr, kr, vr, gr = pl.pallas_call(
        _proj_kernel,
        grid=(S // TR, B),
        in_specs=[nat_spec(D), whole((1, D)),
                  pl.BlockSpec(w_in.shape, lambda m, b: (0, 0), pipeline_mode=pl.Buffered(1)),
                  whole((1, RET_V_WIDTH)),
                  tab_spec, tab_spec, tab_spec, tab_spec],
        out_specs=[cm_spec, cm_spec, cm_spec, r16_spec(TR), r16_spec(TR), r16_spec(TR), nat_spec(ATTN_WIDTH),
                   nat_spec(RET_QK_WIDTH), nat_spec(RET_QK_WIDTH), nat_spec(RET_V_WIDTH), nat_spec(RET_V_WIDTH)],
        out_shape=[jax.ShapeDtypeStruct(cm_shape, bf16)] * 3 + [jax.ShapeDtypeStruct(r16_shape, bf16)] * 3 + [
            jax.ShapeDtypeStruct((B, S, ATTN_WIDTH), bf16),
            jax.ShapeDtypeStruct((B, S, RET_QK_WIDTH), bf16), jax.ShapeDtypeStruct((B, S, RET_QK_WIDTH), bf16),
            jax.ShapeDtypeStruct((B, S, RET_V_WIDTH), bf16), jax.ShapeDtypeStruct((B, S, RET_V_WIDTH), bf16),
        ],
        scratch_shapes=[pltpu.VMEM((3 * ATTN_PAIRS, PROJ_SUB_ROWS, LANES), f32),
                        pltpu.VMEM((3 * ATTN_PAIRS, CLASSES, PROJ_SUB_ROWS // CLASSES, LANES), f32)],
        compiler_params=params(dimension_semantics=two_arb),
        name="proj_rope",
    )(x, norm_gain.reshape(1, D), w_in.astype(bf16), ret_gn_gain.reshape(1, RET_V_WIDTH), ca, sa, cr, sr)

    cm_seq = pl.BlockSpec((None, None, CLASSES, L, LANES), lambda b, hp: (b, hp, 0, 0, 0))
    r16_seq = pl.BlockSpec((None, None, MAX_DILATION, S // MAX_DILATION, LANES), lambda b, hp: (b, hp, 0, 0, 0))
    bias, ones = _attention_tables()
    attn = pl.pallas_call(
        _attn_kernel,
        grid=(B, ATTN_PAIRS),
        in_specs=[cm_seq, cm_seq, cm_seq, r16_seq, r16_seq, r16_seq, whole(bias.shape), whole(ones.shape)],
        out_specs=r16_seq,
        out_shape=jax.ShapeDtypeStruct(r16_shape, bf16),
        scratch_shapes=[pltpu.VMEM((CLASSES, L, LANES), f32)] * 3
        + [pltpu.VMEM((G, 2, BLOCK, 2 * BLOCK), f32), pltpu.VMEM((G, 2, BLOCK, 2 * BLOCK), bf16),
           pltpu.VMEM((G, 2, BLOCK, LANES), f32), pltpu.VMEM((G, BLOCK, LANES), f32)],
        compiler_params=params(dimension_semantics=two_arb),
        name="dilated_attention",
    )(qc, kc, vc, q16, k16, v16, bias, ones)

    pairs_r = RET_HEADS // 2
    C = RET_CHUNK
    assert S % (C * RET_CHUNKS_PER_STEP) == 0
    qk_spec = pl.BlockSpec((None, S, 2 * RET_KEY_DIM), lambda b, hp: (b, 0, hp))
    v_spec = pl.BlockSpec((None, S, 2 * RET_VALUE_DIM), lambda b, hp: (b, 0, hp))
    ret = pl.pallas_call(
        _ret_kernel,
        grid=(B, pairs_r),
        in_specs=[qk_spec, qk_spec, v_spec, v_spec,
                  pl.BlockSpec((1, 2, C, C), lambda b, hp: (hp, 0, 0, 0)),
                  pl.BlockSpec((1, C, 2 * RET_VALUE_DIM), lambda b, hp: (hp, 0, 0)),
                  pl.BlockSpec((1, C, 2 * RET_KEY_DIM), lambda b, hp: (hp, 0, 0)),
                  pl.BlockSpec((1, 2 * RET_KEY_DIM, 2 * RET_VALUE_DIM), lambda b, hp: (hp, 0, 0))],
        out_specs=v_spec,
        out_shape=jax.ShapeDtypeStruct((B, S, RET_V_WIDTH), bf16),
        scratch_shapes=[pltpu.VMEM((2 * RET_KEY_DIM, 2 * RET_VALUE_DIM), f32)],
        compiler_params=params(dimension_semantics=two_arb),
        name="retention",
    )(qr, kr, vr, gr, dmat, qdec, kdec, cdec)

    TO = OUT_ROW_TILE
    return pl.pallas_call(
        _out_kernel,
        grid=(S // TO, B),
        in_specs=[nat_spec(D, TO), r16_spec(TO), nat_spec(ATTN_WIDTH, TO), nat_spec(RET_V_WIDTH, TO),
                  whole(w_out.shape), whole((1, D))],
        out_specs=nat_spec(D, TO),
        out_shape=jax.ShapeDtypeStruct((B, S, D), f32),
        scratch_shapes=[pltpu.VMEM((ATTN_PAIRS, TO, LANES), f32),
                        pltpu.VMEM((ATTN_PAIRS, CLASSES, TO // CLASSES, LANES), f32)],
        compiler_params=params(dimension_semantics=two_arb),
        name="out_proj_norm",
    )(x, attn, ga, ret, w_out.astype(bf16), out_gain.reshape(1, D))


def kernel(x, norm_gain, w_in, ret_gn_gain, w_out, final_gain):
    assert norm_gain.shape[0] == 1, "the fused output projection + final norm assumes a single layer"
    assert x.shape[1] % (MAX_DILATION * BLOCK) == 0
    return _mixer_layer(x, norm_gain[0], w_in[0], ret_gn_gain[0], w_out[0], final_gain)
```

```python
import functools
import math

import numpy as np
import jax
import jax.numpy as jnp
from jax import lax
from jax.experimental import pallas as pl
from jax.experimental.pallas import tpu as pltpu

LANES = 128

ATTN_HEADS = 8
ATTN_HEAD_DIM = 64
ATTN_WIDTH = ATTN_HEADS * ATTN_HEAD_DIM
ATTN_PAIRS = ATTN_WIDTH // LANES
RET_HEADS = 4
RET_KEY_DIM = 64
RET_VALUE_DIM = 128
RET_QK_WIDTH = RET_HEADS * RET_KEY_DIM
RET_V_WIDTH = RET_HEADS * RET_VALUE_DIM
RET_CHUNK = 128
ROPE_THETA = 10000.0
NORM_EPS = 1e-6
GN_EPS = 1e-5

CLASSES = 4
BLOCK = 128
MAX_DILATION = 16
QUADS = MAX_DILATION // CLASSES
NEG_BIG = -1e30

VMEM_LIMIT_BYTES = 52 * 1024 * 1024
ROW_TILE = 1024
PROJ_SUB_ROWS = 512
OUT_ROW_TILE = 1024
BLOCKS_PER_STEP = 4
RET_CHUNKS_PER_STEP = 8


def _silu(g):
    return g / (1.0 + jnp.exp(-g))


def _proj_kernel(x_ref, gain_ref, w_ref, gn_gain_ref, ca_ref, sa_ref, cr_ref, sr_ref,
                 qc_ref, kc_ref, vc_ref, q16_ref, k16_ref, v16_ref, ga_ref, qr_ref, kr_ref, vr_ref, gr_ref,
                 perm_s, perm2_s):
    rows = PROJ_SUB_ROWS
    lane = lax.broadcasted_iota(jnp.int32, (rows, LANES), 1)
    first_half = (lane % ATTN_HEAD_DIM) < (ATTN_HEAD_DIM // 2)
    even = (lane % 2) == 0
    bf16 = jnp.bfloat16

    def rope_half(slab, cos, sin):
        fwd = pltpu.roll(slab, LANES - ATTN_HEAD_DIM // 2, 1)
        bwd = pltpu.roll(slab, ATTN_HEAD_DIM // 2, 1)
        return slab * cos + jnp.where(first_half, fwd, bwd) * sin

    def rope_pairs(slab, cos, sin):
        fwd = pltpu.roll(slab, LANES - 1, 1)
        bwd = pltpu.roll(slab, 1, 1)
        return slab * cos + jnp.where(even, fwd, bwd) * sin

    def proj(h, lo, width):
        return jnp.dot(h, w_ref[:, lo:lo + width], preferred_element_type=jnp.float32)

    for sub in range(x_ref.shape[1] // rows):
        rs = pl.ds(sub * rows, rows)

        def store_permuted(cm_ref, r16_ref, hp, slot, slab):
            perm_s[slot] = slab
            for c in range(CLASSES):
                cls = perm_s[slot, pl.ds(c, rows // CLASSES, stride=CLASSES), :]
                cm_ref[0, hp, c, pl.ds(sub * rows // CLASSES, rows // CLASSES), :] = cls.astype(bf16)
                perm2_s[slot, c] = cls
                for a in range(QUADS):
                    r16_ref[0, hp, QUADS * a + c, pl.ds(sub * rows // MAX_DILATION, rows // MAX_DILATION), :] = (
                        perm2_s[slot, c, pl.ds(a, rows // MAX_DILATION, stride=QUADS), :].astype(bf16))

        x = x_ref[0, rs, :]
        ms = jnp.mean(x * x, axis=-1, keepdims=True)
        h = (x * lax.rsqrt(ms + NORM_EPS) * gain_ref[...]).astype(bf16)
        ca, sa = ca_ref[rs, :], sa_ref[rs, :]
        cr, sr = cr_ref[rs, :], sr_ref[rs, :]

        lo = 0
        pq = proj(h, lo, ATTN_WIDTH); lo += ATTN_WIDTH
        pk = proj(h, lo, ATTN_WIDTH); lo += ATTN_WIDTH
        pv = proj(h, lo, ATTN_WIDTH); lo += ATTN_WIDTH
        pg = proj(h, lo, ATTN_WIDTH); lo += ATTN_WIDTH
        q_scale = math.log2(math.e) * ATTN_HEAD_DIM ** -0.5
        for hp in range(ATTN_PAIRS):
            ls = slice(hp * LANES, (hp + 1) * LANES)
            store_permuted(qc_ref, q16_ref, hp, hp, rope_half(pq[:, ls], ca, sa) * q_scale)
            store_permuted(kc_ref, k16_ref, hp, ATTN_PAIRS + hp, rope_half(pk[:, ls], ca, sa))
            store_permuted(vc_ref, v16_ref, hp, 2 * ATTN_PAIRS + hp, pv[:, ls])
        ga_ref[0, rs, :] = _silu(pg).astype(bf16)

        pq = proj(h, lo, RET_QK_WIDTH); lo += RET_QK_WIDTH
        pk = proj(h, lo, RET_QK_WIDTH); lo += RET_QK_WIDTH
        for hp in range(RET_QK_WIDTH // LANES):
            ls = slice(hp * LANES, (hp + 1) * LANES)
            qr_ref[0, rs, ls] = rope_pairs(pq[:, ls], cr, sr).astype(bf16)
            kr_ref[0, rs, ls] = (rope_pairs(pk[:, ls], cr, sr) * (RET_KEY_DIM ** -0.5)).astype(bf16)
        pv = proj(h, lo, RET_V_WIDTH); lo += RET_V_WIDTH
        pg = proj(h, lo, RET_V_WIDTH); lo += RET_V_WIDTH
        vr_ref[0, rs, :] = pv.astype(bf16)
        gr_ref[0, rs, :] = (_silu(pg) * gn_gain_ref[...]).astype(bf16)


def _attn_kernel(qc_ref, kc_ref, vc_ref, q16_ref, k16_ref, v16_ref, bias_ref, ones_ref, o_ref,
                 acc_s, m_s, den_s, p_s, mpair_s):
    G = BLOCKS_PER_STEP
    L = qc_ref.shape[1]
    lane = lax.broadcasted_iota(jnp.int32, (BLOCK, LANES), 1)
    head0 = lane < ATTN_HEAD_DIM
    head_masks = (head0, jnp.logical_not(head0))
    bf16 = jnp.bfloat16
    state = (acc_s, m_s, den_s)
    sub = BLOCK // CLASSES
    blocks3 = L // QUADS // BLOCK

    def twin(ref):
        return q16_ref if ref is qc_ref else k16_ref if ref is kc_ref else v16_ref

    def is_static(v):
        return isinstance(v, (int, bool))

    def aligned(n, size):
        return n * size if is_static(n) else pl.multiple_of(n * size, size)

    def segments(branch, t):
        if branch == 0:
            def load(ref, n):
                return jnp.concatenate([ref[c, pl.ds(aligned(n, sub), sub), :] for c in range(CLASSES)], axis=0)

            def store(ref, n, val):
                for c in range(CLASSES):
                    ref[c, pl.ds(aligned(n, sub), sub), :] = val[c * sub:(c + 1) * sub, :]

            n0 = t * G
            prev = max(n0 - 1, 0) if is_static(n0) else jnp.maximum(n0 - 1, 0)
            return [(load, load, store, [n0 + g for g in range(G)], prev, n0 == 0)]
        if branch == 1:
            per_class = L // BLOCK // G
            c = t // per_class
            n0 = (t % per_class) * G

            def load(ref, n):
                return ref[c, pl.ds(aligned(n, BLOCK), BLOCK), :]

            def store(ref, n, val):
                ref[c, pl.ds(aligned(n, BLOCK), BLOCK), :] = val

            prev = max(n0 - 1, 0) if is_static(n0) else jnp.maximum(n0 - 1, 0)
            return [(load, load, store, [n0 + g for g in range(G)], prev, n0 == 0)]
        segs = []
        for j in range(G // blocks3):
            r = t * (G // blocks3) + j
            c = r % CLASSES
            a = r // CLASSES

            def load_in(ref, n, r=r):
                return twin(ref)[r, pl.ds(n * BLOCK, BLOCK), :]

            def load_state(ref, n, c=c, a=a):
                return ref[c, pl.ds(n * BLOCK * QUADS + a, BLOCK, stride=QUADS), :]

            def store(ref, n, val, r=r, c=c, a=a):
                if ref is o_ref:
                    o_ref[r, pl.ds(n * BLOCK, BLOCK), :] = val.astype(o_ref.dtype)
                else:
                    ref[c, pl.ds(n * BLOCK * QUADS + a, BLOCK, stride=QUADS), :] = val

            segs.append((load_in, load_state, store, list(range(blocks3)), 0, True))
        return segs

    n_groups = (CLASSES * L // BLOCK // G, CLASSES * L // BLOCK // G, MAX_DILATION * blocks3 // G)

    def stage_probs(branch, t):
        i = 0
        for load_in, _, _, blocks, prev, first in segments(branch, t):
            kb = [load_in(kc_ref, nb) for nb in [prev] + blocks]
            for g, n in enumerate(blocks):
                bias = bias_ref[branch, 1 if (g == 0 and first) else 0]
                keys = jnp.concatenate([kb[g], kb[g + 1]], axis=0)
                q = load_in(qc_ref, n)
                ms = []
                for h in range(2):
                    qh = q * ones_ref[h]
                    s = lax.dot_general(qh, keys, (((1,), (1,)), ((), ())),
                                        preferred_element_type=jnp.float32) + bias
                    m = jnp.max(s, axis=1, keepdims=True)
                    p_s[i, h] = jnp.exp2(s - m).astype(bf16)
                    ms.append(m)
                mpair_s[i] = jnp.where(head0, ms[0], ms[1])
                i += 1

    def stage_values(branch, t):
        i = 0
        for load_in, load_state, store, blocks, prev, _ in segments(branch, t):
            wb = []
            for nb in [prev] + blocks:
                v = load_in(vc_ref, nb)
                wb.append([jnp.concatenate([v * ones_ref[h], ones_ref[h]], axis=1)
                           for h in range(2)])
            for g, n in enumerate(blocks):
                weights = jnp.concatenate([wb[g][0], wb[g + 1][0], wb[g][1], wb[g + 1][1]], axis=0)
                probs = jnp.concatenate([p_s[i, 0], p_s[i, 1]], axis=1)
                r = jnp.dot(probs, weights, preferred_element_type=jnp.float32)
                acc, den = r[:, :LANES], r[:, LANES:]
                m = mpair_s[i]
                i += 1
                if branch > 0:
                    acc_o, m_o, den_o = (load_state(ref, n) for ref in state)
                    m_n, m = m, jnp.maximum(m_o, m)
                    w_o = jnp.exp2(m_o - m)
                    w_n = jnp.exp2(m_n - m)
                    acc = w_o * acc_o + w_n * acc
                    den = w_o * den_o + w_n * den
                if branch == 2:
                    store(o_ref, n, acc / den)
                else:
                    for ref, val in zip(state, (acc, m, den)):
                        store(ref, n, val)

    per_branch = n_groups[0]
    assert n_groups == (per_branch,) * 3
    total = 3 * per_branch
    for tau in range(total + 1):
        if tau >= 1:
            stage_values((tau - 1) // per_branch, (tau - 1) % per_branch)
        if tau < total:
            stage_probs(tau // per_branch, tau % per_branch)


def _ret_kernel(q_ref, k_ref, v_ref, g_ref, dmat_ref, qdec_ref, kdec_ref, cdec_ref, o_ref, state_s):
    C = RET_CHUNK
    G = RET_CHUNKS_PER_STEP
    lane = lax.broadcasted_iota(jnp.int32, (C, LANES), 1)
    head0 = lane < RET_KEY_DIM
    head_masks = (head0, jnp.logical_not(head0))
    bf16 = jnp.bfloat16
    state_s[...] = jnp.zeros_like(state_s)

    def group(i, carry):
        offs = [pl.multiple_of((i * G + j) * C, C) for j in range(G)]
        qh, intra, kvs = [], [], []
        for off in offs:
            q = q_ref[pl.ds(off, C), :]
            k = k_ref[pl.ds(off, C), :]
            v = v_ref[pl.ds(off, C), :]
            qh.append([jnp.where(head_masks[h], q, jnp.zeros_like(q)) for h in range(2)])
            o = []
            for h in range(2):
                s = lax.dot_general(qh[-1][h], k, (((1,), (1,)), ((), ())),
                                    preferred_element_type=jnp.float32) * dmat_ref[0, h]
                vs = slice(h * RET_VALUE_DIM, (h + 1) * RET_VALUE_DIM)
                o.append(jnp.dot(s.astype(bf16), v[:, vs], preferred_element_type=jnp.float32))
            intra.append(o)
            kd = (k.astype(jnp.float32) * kdec_ref[0]).astype(bf16)
            kvs.append(lax.dot_general(kd, v, (((0,), (0,)), ((), ())), preferred_element_type=jnp.float32))
        state = state_s[...]
        states = []
        for kv in kvs:
            states.append(state.astype(bf16))
            state = state * cdec_ref[0] + kv
        state_s[...] = state
        for j, off in enumerate(offs):
            outs = []
            for h in range(2):
                vs = slice(h * RET_VALUE_DIM, (h + 1) * RET_VALUE_DIM)
                inter = jnp.dot(qh[j][h], states[j][:, vs], preferred_element_type=jnp.float32)
                o = intra[j][h] + inter * qdec_ref[0, :, vs]
                mu = jnp.mean(o, axis=-1, keepdims=True)
                d = o - mu
                var = jnp.mean(d * d, axis=-1, keepdims=True)
                outs.append(d * lax.rsqrt(var + GN_EPS) * g_ref[pl.ds(off, C), vs].astype(jnp.float32))
            o_ref[pl.ds(off, C), :] = jnp.concatenate(outs, axis=1).astype(o_ref.dtype)
        return carry

    lax.fori_loop(0, q_ref.shape[0] // C // G, group, 0)


def _out_kernel(x_ref, a_ref, ga_ref, r_ref, w_ref, gain_ref, o_ref, perm_s, perm2_s):
    rows = x_ref.shape[1]
    for hp in range(ATTN_PAIRS):
        for c in range(CLASSES):
            for a in range(QUADS):
                perm2_s[hp, c, pl.ds(a, rows // MAX_DILATION, stride=QUADS), :] = (
                    a_ref[0, hp, QUADS * a + c].astype(jnp.float32))
            perm_s[hp, pl.ds(c, rows // CLASSES, stride=CLASSES), :] = perm2_s[hp, c]
    attn = jnp.concatenate([perm_s[hp] for hp in range(ATTN_PAIRS)], axis=1)
    gated = (attn * ga_ref[0].astype(jnp.float32)).astype(jnp.bfloat16)
    mixed = jnp.concatenate([gated, r_ref[0]], axis=1)
    y = x_ref[0] + jnp.dot(mixed, w_ref[...], preferred_element_type=jnp.float32)
    ms = jnp.mean(y * y, axis=-1, keepdims=True)
    o_ref[0] = y * lax.rsqrt(ms + NORM_EPS) * gain_ref[...]


def _rotation_tables(seq):
    pos = jnp.arange(seq).astype(jnp.float32)
    half = ATTN_HEAD_DIM // 2
    inv = ROPE_THETA ** (-jnp.arange(half, dtype=jnp.float32) / half)
    ang = pos[:, None] * inv[None, :]
    cos, sin = jnp.cos(ang), jnp.sin(ang)
    ca = jnp.tile(jnp.concatenate([cos, cos], axis=1), (1, LANES // ATTN_HEAD_DIM))
    sa = jnp.tile(jnp.concatenate([-sin, sin], axis=1), (1, LANES // ATTN_HEAD_DIM))
    half = RET_KEY_DIM // 2
    inv = 1.0 / (ROPE_THETA ** jnp.linspace(0.0, 1.0, half, dtype=jnp.float32))
    ang = pos[:, None] * inv[None, :]
    cos, sin = jnp.cos(ang), jnp.sin(ang)
    cr = jnp.tile(jnp.repeat(cos, 2, axis=1), (1, LANES // RET_KEY_DIM))
    sr = jnp.tile(jnp.stack([-sin, sin], axis=-1).reshape(seq, RET_KEY_DIM), (1, LANES // RET_KEY_DIM))
    return ca, sa, cr, sr


def _attention_tables():
    i = np.arange(BLOCK)
    sz = BLOCK // CLASSES
    orders = [
        CLASSES * (i % sz) + i // sz,
        i,
        i,
    ]
    bias = np.zeros((3, 2, BLOCK, 2 * BLOCK), np.float32)
    for b, u in enumerate(orders):
        prev_ok = u[None, :] >= u[:, None]
        cur_ok = u[None, :] <= u[:, None]
        bias[b, 0] = np.where(np.concatenate([prev_ok, cur_ok], axis=1), 0.0, NEG_BIG)
        bias[b, 1] = np.where(np.concatenate([np.zeros_like(prev_ok), cur_ok], axis=1), 0.0, NEG_BIG)
    lane_head = np.arange(LANES) // ATTN_HEAD_DIM
    ones = np.broadcast_to((lane_head[None, :] == np.arange(2)[:, None])[:, None, :], (2, BLOCK, LANES))
    return jnp.asarray(bias), jnp.asarray(ones, dtype=jnp.bfloat16)


def _retention_tables():
    C = RET_CHUNK
    log_gamma = jnp.log1p(-(2.0 ** (-5.0 - jnp.arange(RET_HEADS, dtype=jnp.float32))))
    cpos = jnp.arange(C, dtype=jnp.float32)
    diff = cpos[:, None] - cpos[None, :]
    dmat = jnp.where(diff[None] >= 0, jnp.exp(jnp.maximum(diff, 0.0)[None] * log_gamma[:, None, None]), 0.0)
    k_dec = jnp.exp((C - 1 - cpos)[None, :] * log_gamma[:, None])
    q_dec = jnp.exp((cpos + 1)[None, :] * log_gamma[:, None])
    chunk_decay = jnp.exp(C * log_gamma)
    pairs = RET_HEADS // 2
    dmat = dmat.reshape(pairs, 2, C, C)
    qdec = jnp.repeat(q_dec.reshape(pairs, 2, C).transpose(0, 2, 1), RET_VALUE_DIM, axis=2)
    kdec = jnp.repeat(k_dec.reshape(pairs, 2, C).transpose(0, 2, 1), RET_KEY_DIM, axis=2)
    row_head = jnp.arange(2 * RET_KEY_DIM) // RET_KEY_DIM
    cdec = jnp.broadcast_to(chunk_decay.reshape(pairs, 2)[:, row_head][:, :, None],
                            (pairs, 2 * RET_KEY_DIM, 2 * RET_VALUE_DIM))
    return dmat, qdec, kdec, cdec


def _mixer_layer(x, norm_gain, w_in, ret_gn_gain, w_out, out_gain):
    B, S, D = x.shape
    L = S // CLASSES
    TR = ROW_TILE
    G = BLOCKS_PER_STEP
    f32, bf16 = jnp.float32, jnp.bfloat16
    params = functools.partial(pltpu.CompilerParams, vmem_limit_bytes=VMEM_LIMIT_BYTES)
    two_arb = ("arbitrary", "arbitrary")

    with jax.ensure_compile_time_eval(), jax.default_device(jax.devices("cpu")[0]):
        ca, sa, cr, sr = _rotation_tables(S)
        dmat, qdec, kdec, cdec = _retention_tables()
    tab_spec = pl.BlockSpec((TR, LANES), lambda m, b: (m, 0))
    cm_shape = (B, ATTN_PAIRS, CLASSES, L, LANES)
    cm_spec = pl.BlockSpec((1, ATTN_PAIRS, CLASSES, TR // CLASSES, LANES), lambda m, b: (b, 0, 0, m, 0))
    r16_shape = (B, ATTN_PAIRS, MAX_DILATION, S // MAX_DILATION, LANES)

    def r16_spec(tr):
        return pl.BlockSpec((1, ATTN_PAIRS, MAX_DILATION, tr // MAX_DILATION, LANES), lambda m, b: (b, 0, 0, m, 0))

    def nat_spec(width, tr=TR):
        return pl.BlockSpec((1, tr, width), lambda m, b: (b, m, 0))

    def whole(shape):
        return pl.BlockSpec(shape, lambda *_: (0,) * len(shape))

    qc, kc, vc, q16, k16, v16, ga, qr, kr, vr, gr = pl.pallas_call(
        _proj_kernel,
        grid=(S // TR, B),
        in_specs=[nat_spec(D), whole((1, D)),
                  pl.BlockSpec(w_in.shape, lambda m, b: (0, 0), pipeline_mode=pl.Buffered(1)),
                  whole((1, RET_V_WIDTH)),
                  tab_spec, tab_spec, tab_spec, tab_spec],
        out_specs=[cm_spec, cm_spec, cm_spec, r16_spec(TR), r16_spec(TR), r16_spec(TR), nat_spec(ATTN_WIDTH),
                   nat_spec(RET_QK_WIDTH), nat_spec(RET_QK_WIDTH), nat_spec(RET_V_WIDTH), nat_spec(RET_V_WIDTH)],
        out_shape=[jax.ShapeDtypeStruct(cm_shape, bf16)] * 3 + [jax.ShapeDtypeStruct(r16_shape, bf16)] * 3 + [
            jax.ShapeDtypeStruct((B, S, ATTN_WIDTH), bf16),
            jax.ShapeDtypeStruct((B, S, RET_QK_WIDTH), bf16), jax.ShapeDtypeStruct((B, S, RET_QK_WIDTH), bf16),
            jax.ShapeDtypeStruct((B, S, RET_V_WIDTH), bf16), jax.ShapeDtypeStruct((B, S, RET_V_WIDTH), bf16),
        ],
        scratch_shapes=[pltpu.VMEM((3 * ATTN_PAIRS, PROJ_SUB_ROWS, LANES), f32),
                        pltpu.VMEM((3 * ATTN_PAIRS, CLASSES, PROJ_SUB_ROWS // CLASSES, LANES), f32)],
        compiler_params=params(dimension_semantics=two_arb),
        name="proj_rope",
    )(x, norm_gain.reshape(1, D), w_in.astype(bf16), ret_gn_gain.reshape(1, RET_V_WIDTH), ca, sa, cr, sr)

    cm_seq = pl.BlockSpec((None, None, CLASSES, L, LANES), lambda b, hp: (b, hp, 0, 0, 0))
    r16_seq = pl.BlockSpec((None, None, MAX_DILATION, S // MAX_DILATION, LANES), lambda b, hp: (b, hp, 0, 0, 0))
    bias, ones = _attention_tables()
    attn = pl.pallas_call(
        _attn_kernel,
        grid=(B, ATTN_PAIRS),
        in_specs=[cm_seq, cm_seq, cm_seq, r16_seq, r16_seq, r16_seq, whole(bias.shape), whole(ones.shape)],
        out_specs=r16_seq,
        out_shape=jax.ShapeDtypeStruct(r16_shape, bf16),
        scratch_shapes=[pltpu.VMEM((CLASSES, L, LANES), f32)] * 3
        + [pltpu.VMEM((G, 2, BLOCK, 2 * BLOCK), bf16), pltpu.VMEM((G, BLOCK, LANES), f32)],
        compiler_params=params(dimension_semantics=two_arb),
        name="dilated_attention",
    )(qc, kc, vc, q16, k16, v16, bias, ones)

    pairs_r = RET_HEADS // 2
    C = RET_CHUNK
    assert S % (C * RET_CHUNKS_PER_STEP) == 0
    qk_spec = pl.BlockSpec((None, S, 2 * RET_KEY_DIM), lambda b, hp: (b, 0, hp))
    v_spec = pl.BlockSpec((None, S, 2 * RET_VALUE_DIM), lambda b, hp: (b, 0, hp))
    ret = pl.pallas_call(
        _ret_kernel,
        grid=(B, pairs_r),
        in_specs=[qk_spec, qk_spec, v_spec, v_spec,
                  pl.BlockSpec((1, 2, C, C), lambda b, hp: (hp, 0, 0, 0)),
                  pl.BlockSpec((1, C, 2 * RET_VALUE_DIM), lambda b, hp: (hp, 0, 0)),
                  pl.BlockSpec((1, C, 2 * RET_KEY_DIM), lambda b, hp: (hp, 0, 0)),
                  pl.BlockSpec((1, 2 * RET_KEY_DIM, 2 * RET_VALUE_DIM), lambda b, hp: (hp, 0, 0))],
        out_specs=v_spec,
        out_shape=jax.ShapeDtypeStruct((B, S, RET_V_WIDTH), bf16),
        scratch_shapes=[pltpu.VMEM((2 * RET_KEY_DIM, 2 * RET_VALUE_DIM), f32)],
        compiler_params=params(dimension_semantics=two_arb),
        name="retention",
    )(qr, kr, vr, gr, dmat, qdec, kdec, cdec)

    TO = OUT_ROW_TILE
    return pl.pallas_call(
        _out_kernel,
        grid=(S // TO, B),
        in_specs=[nat_spec(D, TO), r16_spec(TO), nat_spec(ATTN_WIDTH, TO), nat_spec(RET_V_WIDTH, TO),
                  whole(w_out.shape), whole((1, D))],
        out_specs=nat_spec(D, TO),
        out_shape=jax.ShapeDtypeStruct((B, S, D), f32),
        scratch_shapes=[pltpu.VMEM((ATTN_PAIRS, TO, LANES), f32),
                        pltpu.VMEM((ATTN_PAIRS, CLASSES, TO // CLASSES, LANES), f32)],
        compiler_params=params(dimension_semantics=two_arb),
        name="out_proj_norm",
    )(x, attn, ga, ret, w_out.astype(bf16), out_gain.reshape(1, D))


def kernel(x, norm_gain, w_in, ret_gn_gain, w_out, final_gain):
    assert norm_gain.shape[0] == 1, "the fused output projection + final norm assumes a single layer"
    assert x.shape[1] % (MAX_DILATION * BLOCK) == 0
    return _mixer_layer(x, norm_gain[0], w_in[0], ret_gn_gain[0], w_out[0], final_gain)
```

```python
import functools
import math

import numpy as np
import jax
import jax.numpy as jnp
from jax import lax
from jax.experimental import pallas as pl
from jax.experimental.pallas import tpu as pltpu

LANES = 128

ATTN_HEADS = 8
ATTN_HEAD_DIM = 64
ATTN_WIDTH = ATTN_HEADS * ATTN_HEAD_DIM
ATTN_PAIRS = ATTN_WIDTH // LANES
RET_HEADS = 4
RET_KEY_DIM = 64
RET_VALUE_DIM = 128
RET_QK_WIDTH = RET_HEADS * RET_KEY_DIM
RET_V_WIDTH = RET_HEADS * RET_VALUE_DIM
RET_CHUNK = 128
ROPE_THETA = 10000.0
NORM_EPS = 1e-6
GN_EPS = 1e-5

CLASSES = 4
BLOCK = 128
MAX_DILATION = 16
QUADS = MAX_DILATION // CLASSES
NEG_BIG = -1e30

VMEM_LIMIT_BYTES = 52 * 1024 * 1024
ROW_TILE = 1024
PROJ_SUB_ROWS = 512
OUT_ROW_TILE = 1024
BLOCKS_PER_STEP = 4
RET_CHUNKS_PER_STEP = 8


def _silu(g):
    return g / (1.0 + jnp.exp(-g))


def _proj_kernel(x_ref, gain_ref, w_ref, gn_gain_ref, ca_ref, sa_ref, cr_ref, sr_ref,
                 qc_ref, kc_ref, vc_ref, q16_ref, k16_ref, v16_ref, ga_ref, qr_ref, kr_ref, vr_ref, gr_ref,
                 perm_s, perm2_s):
    rows = PROJ_SUB_ROWS
    lane = lax.broadcasted_iota(jnp.int32, (rows, LANES), 1)
    first_half = (lane % ATTN_HEAD_DIM) < (ATTN_HEAD_DIM // 2)
    even = (lane % 2) == 0
    bf16 = jnp.bfloat16

    def rope_half(slab, cos, sin):
        fwd = pltpu.roll(slab, LANES - ATTN_HEAD_DIM // 2, 1)
        bwd = pltpu.roll(slab, ATTN_HEAD_DIM // 2, 1)
        return slab * cos + jnp.where(first_half, fwd, bwd) * sin

    def rope_pairs(slab, cos, sin):
        fwd = pltpu.roll(slab, LANES - 1, 1)
        bwd = pltpu.roll(slab, 1, 1)
        return slab * cos + jnp.where(even, fwd, bwd) * sin

    def proj(h, lo, width):
        return jnp.dot(h, w_ref[:, lo:lo + width], preferred_element_type=jnp.float32)

    for sub in range(x_ref.shape[1] // rows):
        rs = pl.ds(sub * rows, rows)

        def store_permuted(cm_ref, r16_ref, hp, slot, slab):
            perm_s[slot] = slab
            for c in range(CLASSES):
                cls = perm_s[slot, pl.ds(c, rows // CLASSES, stride=CLASSES), :]
                cm_ref[0, hp, c, pl.ds(sub * rows // CLASSES, rows // CLASSES), :] = cls.astype(bf16)
                perm2_s[slot, c] = cls
                for a in range(QUADS):
                    r16_ref[0, hp, QUADS * a + c, pl.ds(sub * rows // MAX_DILATION, rows // MAX_DILATION), :] = (
                        perm2_s[slot, c, pl.ds(a, rows // MAX_DILATION, stride=QUADS), :].astype(bf16))

        x = x_ref[0, rs, :]
        ms = jnp.mean(x * x, axis=-1, keepdims=True)
        h = (x * lax.rsqrt(ms + NORM_EPS) * gain_ref[...]).astype(bf16)
        ca, sa = ca_ref[rs, :], sa_ref[rs, :]
        cr, sr = cr_ref[rs, :], sr_ref[rs, :]

        lo = 0
        pq = proj(h, lo, ATTN_WIDTH); lo += ATTN_WIDTH
        pk = proj(h, lo, ATTN_WIDTH); lo += ATTN_WIDTH
        pv = proj(h, lo, ATTN_WIDTH); lo += ATTN_WIDTH
        pg = proj(h, lo, ATTN_WIDTH); lo += ATTN_WIDTH
        q_scale = math.log2(math.e) * ATTN_HEAD_DIM ** -0.5
        for hp in range(ATTN_PAIRS):
            ls = slice(hp * LANES, (hp + 1) * LANES)
            store_permuted(qc_ref, q16_ref, hp, hp, rope_half(pq[:, ls], ca, sa) * q_scale)
            store_permuted(kc_ref, k16_ref, hp, ATTN_PAIRS + hp, rope_half(pk[:, ls], ca, sa))
            store_permuted(vc_ref, v16_ref, hp, 2 * ATTN_PAIRS + hp, pv[:, ls])
        ga_ref[0, rs, :] = _silu(pg).astype(bf16)

        pq = proj(h, lo, RET_QK_WIDTH); lo += RET_QK_WIDTH
        pk = proj(h, lo, RET_QK_WIDTH); lo += RET_QK_WIDTH
        for hp in range(RET_QK_WIDTH // LANES):
            ls = slice(hp * LANES, (hp + 1) * LANES)
            qr_ref[0, rs, ls] = rope_pairs(pq[:, ls], cr, sr).astype(bf16)
            kr_ref[0, rs, ls] = (rope_pairs(pk[:, ls], cr, sr) * (RET_KEY_DIM ** -0.5)).astype(bf16)
        pv = proj(h, lo, RET_V_WIDTH); lo += RET_V_WIDTH
        pg = proj(h, lo, RET_V_WIDTH); lo += RET_V_WIDTH
        vr_ref[0, rs, :] = pv.astype(bf16)
        gr_ref[0, rs, :] = (_silu(pg) * gn_gain_ref[...]).astype(bf16)


def _attn_kernel(qc_ref, kc_ref, vc_ref, q16_ref, k16_ref, v16_ref, bias_ref, ones_ref, o_ref,
                 acc_s, m_s, den_s, p_s, mpair_s):
    G = BLOCKS_PER_STEP
    L = qc_ref.shape[1]
    lane = lax.broadcasted_iota(jnp.int32, (BLOCK, LANES), 1)
    head0 = lane < ATTN_HEAD_DIM
    bf16 = jnp.bfloat16
    state = (acc_s, m_s, den_s)
    sub = BLOCK // CLASSES
    blocks3 = L // QUADS // BLOCK

    def twin(ref):
        return q16_ref if ref is qc_ref else k16_ref if ref is kc_ref else v16_ref

    def segments(branch, t):
        if branch == 0:
            def load(ref, n):
                return jnp.concatenate([ref[c, pl.ds(n * sub, sub), :] for c in range(CLASSES)], axis=0)

            def store(ref, n, val):
                for c in range(CLASSES):
                    ref[c, pl.ds(n * sub, sub), :] = val[c * sub:(c + 1) * sub, :]

            return [(load, load, store, [t * G + g for g in range(G)])]
        if branch == 1:
            per_class = L // BLOCK // G
            c = t // per_class

            def load(ref, n):
                return ref[c, pl.ds(n * BLOCK, BLOCK), :]

            def store(ref, n, val):
                ref[c, pl.ds(n * BLOCK, BLOCK), :] = val

            return [(load, load, store, [(t % per_class) * G + g for g in range(G)])]
        segs = []
        for j in range(G // blocks3):
            r = t * (G // blocks3) + j
            c = r % CLASSES
            a = r // CLASSES

            def load_in(ref, n, r=r):
                return twin(ref)[r, pl.ds(n * BLOCK, BLOCK), :]

            def load_state(ref, n, c=c, a=a):
                return ref[c, pl.ds(n * BLOCK * QUADS + a, BLOCK, stride=QUADS), :]

            def store(ref, n, val, r=r, c=c, a=a):
                if ref is o_ref:
                    o_ref[r, pl.ds(n * BLOCK, BLOCK), :] = val.astype(o_ref.dtype)
                else:
                    ref[c, pl.ds(n * BLOCK * QUADS + a, BLOCK, stride=QUADS), :] = val

            segs.append((load_in, load_state, store, list(range(blocks3))))
        return segs

    n_groups = (CLASSES * L // BLOCK // G, CLASSES * L // BLOCK // G, MAX_DILATION * blocks3 // G)

    def key_blocks(blocks):
        return blocks if blocks[0] == 0 else [blocks[0] - 1] + blocks

    def stage_probs(branch, t):
        i = 0
        for load_in, _, _, blocks in segments(branch, t):
            kb = {nb: load_in(kc_ref, nb) for nb in key_blocks(blocks)}
            for n in blocks:
                if n == 0:
                    keys, bias = kb[n], bias_ref[branch, :, pl.ds(BLOCK, BLOCK)]
                else:
                    keys, bias = jnp.concatenate([kb[n - 1], kb[n]], axis=0), bias_ref[branch]
                q = load_in(qc_ref, n)
                ms = []
                for h in range(2):
                    qh = q * ones_ref[h]
                    s = lax.dot_general(qh, keys, (((1,), (1,)), ((), ())),
                                        preferred_element_type=jnp.float32) + bias
                    m = jnp.max(s, axis=1, keepdims=True)
                    p_s[i, h, :, pl.ds(0, keys.shape[0])] = jnp.exp2(s - m).astype(bf16)
                    ms.append(m)
                mpair_s[i] = jnp.where(head0, ms[0], ms[1])
                i += 1

    def stage_values(branch, t):
        i = 0
        for load_in, load_state, store, blocks in segments(branch, t):
            wb = {}
            for nb in key_blocks(blocks):
                v = load_in(vc_ref, nb)
                wb[nb] = [jnp.concatenate([v * ones_ref[h], ones_ref[h]], axis=1) for h in range(2)]
            for n in blocks:
                if n == 0:
                    weights = jnp.concatenate([wb[n][0], wb[n][1]], axis=0)
                    probs = jnp.concatenate([p_s[i, 0, :, pl.ds(0, BLOCK)], p_s[i, 1, :, pl.ds(0, BLOCK)]], axis=1)
                else:
                    weights = jnp.concatenate([wb[n - 1][0], wb[n][0], wb[n - 1][1], wb[n][1]], axis=0)
                    probs = jnp.concatenate([p_s[i, 0], p_s[i, 1]], axis=1)
                r = jnp.dot(probs, weights, preferred_element_type=jnp.float32)
                acc, den = r[:, :LANES], r[:, LANES:]
                m = mpair_s[i]
                i += 1
                if branch > 0:
                    acc_o, m_o, den_o = (load_state(ref, n) for ref in state)
                    m_n, m = m, jnp.maximum(m_o, m)
                    w_o = jnp.exp2(m_o - m)
                    w_n = jnp.exp2(m_n - m)
                    acc = w_o * acc_o + w_n * acc
                    den = w_o * den_o + w_n * den
                if branch == 2:
                    store(o_ref, n, acc / den)
                else:
                    for ref, val in zip(state, (acc, m, den)):
                        store(ref, n, val)

    per_branch = n_groups[0]
    assert n_groups == (per_branch,) * 3
    total = 3 * per_branch
    for tau in range(total + 1):
        if tau >= 1:
            stage_values((tau - 1) // per_branch, (tau - 1) % per_branch)
        if tau < total:
            stage_probs(tau // per_branch, tau % per_branch)


def _ret_kernel(q_ref, k_ref, v_ref, g_ref, dmat_ref, qdec_ref, kdec_ref, cdec_ref, o_ref, state_s):
    C = RET_CHUNK
    G = RET_CHUNKS_PER_STEP
    lane = lax.broadcasted_iota(jnp.int32, (C, LANES), 1)
    head0 = lane < RET_KEY_DIM
    head_masks = (head0, jnp.logical_not(head0))
    bf16 = jnp.bfloat16
    state_s[...] = jnp.zeros_like(state_s)

    def group(i, carry):
        offs = [pl.multiple_of((i * G + j) * C, C) for j in range(G)]
        qh, intra, kvs = [], [], []
        for off in offs:
            q = q_ref[pl.ds(off, C), :]
            k = k_ref[pl.ds(off, C), :]
            v = v_ref[pl.ds(off, C), :]
            qh.append([jnp.where(head_masks[h], q, jnp.zeros_like(q)) for h in range(2)])
            o = []
            for h in range(2):
                s = lax.dot_general(qh[-1][h], k, (((1,), (1,)), ((), ())),
                                    preferred_element_type=jnp.float32) * dmat_ref[0, h]
                vs = slice(h * RET_VALUE_DIM, (h + 1) * RET_VALUE_DIM)
                o.append(jnp.dot(s.astype(bf16), v[:, vs], preferred_element_type=jnp.float32))
            intra.append(o)
            kd = (k.astype(jnp.float32) * kdec_ref[0]).astype(bf16)
            kvs.append(lax.dot_general(kd, v, (((0,), (0,)), ((), ())), preferred_element_type=jnp.float32))
        state = state_s[...]
        states = []
        for kv in kvs:
            states.append(state.astype(bf16))
            state = state * cdec_ref[0] + kv
        state_s[...] = state
        for j, off in enumerate(offs):
            outs = []
            for h in range(2):
                vs = slice(h * RET_VALUE_DIM, (h + 1) * RET_VALUE_DIM)
                inter = jnp.dot(qh[j][h], states[j][:, vs], preferred_element_type=jnp.float32)
                o = intra[j][h] + inter * qdec_ref[0, :, vs]
                mu = jnp.mean(o, axis=-1, keepdims=True)
                d = o - mu
                var = jnp.mean(d * d, axis=-1, keepdims=True)
                outs.append(d * lax.rsqrt(var + GN_EPS) * g_ref[pl.ds(off, C), vs].astype(jnp.float32))
            o_ref[pl.ds(off, C), :] = jnp.concatenate(outs, axis=1).astype(o_ref.dtype)
        return carry

    lax.fori_loop(0, q_ref.shape[0] // C // G, group, 0)


def _out_kernel(x_ref, a_ref, ga_ref, r_ref, w_ref, gain_ref, o_ref, perm_s, perm2_s):
    rows = x_ref.shape[1]
    for hp in range(ATTN_PAIRS):
        for c in range(CLASSES):
            for a in range(QUADS):
                perm2_s[hp, c, pl.ds(a, rows // MAX_DILATION, stride=QUADS), :] = (
                    a_ref[0, hp, QUADS * a + c].astype(jnp.float32))
            perm_s[hp, pl.ds(c, rows // CLASSES, stride=CLASSES), :] = perm2_s[hp, c]
    attn = jnp.concatenate([perm_s[hp] for hp in range(ATTN_PAIRS)], axis=1)
    gated = (attn * ga_ref[0].astype(jnp.float32)).astype(jnp.bfloat16)
    mixed = jnp.concatenate([gated, r_ref[0]], axis=1)
    y = x_ref[0] + jnp.dot(mixed, w_ref[...], preferred_element_type=jnp.float32)
    ms = jnp.mean(y * y, axis=-1, keepdims=True)
    o_ref[0] = y * lax.rsqrt(ms + NORM_EPS) * gain_ref[...]


def _rotation_tables(seq):
    pos = jnp.arange(seq).astype(jnp.float32)
    half = ATTN_HEAD_DIM // 2
    inv = ROPE_THETA ** (-jnp.arange(half, dtype=jnp.float32) / half)
    ang = pos[:, None] * inv[None, :]
    cos, sin = jnp.cos(ang), jnp.sin(ang)
    ca = jnp.tile(jnp.concatenate([cos, cos], axis=1), (1, LANES // ATTN_HEAD_DIM))
    sa = jnp.tile(jnp.concatenate([-sin, sin], axis=1), (1, LANES // ATTN_HEAD_DIM))
    half = RET_KEY_DIM // 2
    inv = 1.0 / (ROPE_THETA ** jnp.linspace(0.0, 1.0, half, dtype=jnp.float32))
    ang = pos[:, None] * inv[None, :]
    cos, sin = jnp.cos(ang), jnp.sin(ang)
    cr = jnp.tile(jnp.repeat(cos, 2, axis=1), (1, LANES // RET_KEY_DIM))
    sr = jnp.tile(jnp.stack([-sin, sin], axis=-1).reshape(seq, RET_KEY_DIM), (1, LANES // RET_KEY_DIM))
    return ca, sa, cr, sr


def _attention_tables():
    i = np.arange(BLOCK)
    sz = BLOCK // CLASSES
    orders = [
        CLASSES * (i % sz) + i // sz,
        i,
        i,
    ]
    bias = np.zeros((3, BLOCK, 2 * BLOCK), np.float32)
    for b, u in enumerate(orders):
        prev_ok = u[None, :] >= u[:, None]
        cur_ok = u[None, :] <= u[:, None]
        bias[b] = np.where(np.concatenate([prev_ok, cur_ok], axis=1), 0.0, NEG_BIG)
    lane_head = np.arange(LANES) // ATTN_HEAD_DIM
    ones = np.broadcast_to((lane_head[None, :] == np.arange(2)[:, None])[:, None, :], (2, BLOCK, LANES))
    return jnp.asarray(bias), jnp.asarray(ones, dtype=jnp.bfloat16)


def _retention_tables():
    C = RET_CHUNK
    log_gamma = jnp.log1p(-(2.0 ** (-5.0 - jnp.arange(RET_HEADS, dtype=jnp.float32))))
    cpos = jnp.arange(C, dtype=jnp.float32)
    diff = cpos[:, None] - cpos[None, :]
    dmat = jnp.where(diff[None] >= 0, jnp.exp(jnp.maximum(diff, 0.0)[None] * log_gamma[:, None, None]), 0.0)
    k_dec = jnp.exp((C - 1 - cpos)[None, :] * log_gamma[:, None])
    q_dec = jnp.exp((cpos + 1)[None, :] * log_gamma[:, None])
    chunk_decay = jnp.exp(C * log_gamma)
    pairs = RET_HEADS // 2
    dmat = dmat.reshape(pairs, 2, C, C)
    qdec = jnp.repeat(q_dec.reshape(pairs, 2, C).transpose(0, 2, 1), RET_VALUE_DIM, axis=2)
    kdec = jnp.repeat(k_dec.reshape(pairs, 2, C).transpose(0, 2, 1), RET_KEY_DIM, axis=2)
    row_head = jnp.arange(2 * RET_KEY_DIM) // RET_KEY_DIM
    cdec = jnp.broadcast_to(chunk_decay.reshape(pairs, 2)[:, row_head][:, :, None],
                            (pairs, 2 * RET_KEY_DIM, 2 * RET_VALUE_DIM))
    return dmat, qdec, kdec, cdec


def _mixer_layer(x, norm_gain, w_in, ret_gn_gain, w_out, out_gain):
    B, S, D = x.shape
    L = S // CLASSES
    TR = ROW_TILE
    G = BLOCKS_PER_STEP
    f32, bf16 = jnp.float32, jnp.bfloat16
    params = functools.partial(pltpu.CompilerParams, vmem_limit_bytes=VMEM_LIMIT_BYTES)
    two_arb = ("arbitrary", "arbitrary")

    with jax.ensure_compile_time_eval(), jax.default_device(jax.devices("cpu")[0]):
        ca, sa, cr, sr = _rotation_tables(S)
        dmat, qdec, kdec, cdec = _retention_tables()
    tab_spec = pl.BlockSpec((TR, LANES), lambda m, b: (m, 0))
    cm_shape = (B, ATTN_PAIRS, CLASSES, L, LANES)
    cm_spec = pl.BlockSpec((1, ATTN_PAIRS, CLASSES, TR // CLASSES, LANES), lambda m, b: (b, 0, 0, m, 0))
    r16_shape = (B, ATTN_PAIRS, MAX_DILATION, S // MAX_DILATION, LANES)

    def r16_spec(tr):
        return pl.BlockSpec((1, ATTN_PAIRS, MAX_DILATION, tr // MAX_DILATION, LANES), lambda m, b: (b, 0, 0, m, 0))

    def nat_spec(width, tr=TR):
        return pl.BlockSpec((1, tr, width), lambda m, b: (b, m, 0))

    def whole(shape):
        return pl.BlockSpec(shape, lambda *_: (0,) * len(shape))

    qc, kc, vc, q16, k16, v16, ga, qr, kr, vr, gr = pl.pallas_call(
        _proj_kernel,
        grid=(S // TR, B),
        in_specs=[nat_spec(D), whole((1, D)),
                  pl.BlockSpec(w_in.shape, lambda m, b: (0, 0), pipeline_mode=pl.Buffered(1)),
                  whole((1, RET_V_WIDTH)),
                  tab_spec, tab_spec, tab_spec, tab_spec],
        out_specs=[cm_spec, cm_spec, cm_spec, r16_spec(TR), r16_spec(TR), r16_spec(TR), nat_spec(ATTN_WIDTH),
                   nat_spec(RET_QK_WIDTH), nat_spec(RET_QK_WIDTH), nat_spec(RET_V_WIDTH), nat_spec(RET_V_WIDTH)],
        out_shape=[jax.ShapeDtypeStruct(cm_shape, bf16)] * 3 + [jax.ShapeDtypeStruct(r16_shape, bf16)] * 3 + [
            jax.ShapeDtypeStruct((B, S, ATTN_WIDTH), bf16),
            jax.ShapeDtypeStruct((B, S, RET_QK_WIDTH), bf16), jax.ShapeDtypeStruct((B, S, RET_QK_WIDTH), bf16),
            jax.ShapeDtypeStruct((B, S, RET_V_WIDTH), bf16), jax.ShapeDtypeStruct((B, S, RET_V_WIDTH), bf16),
        ],
        scratch_shapes=[pltpu.VMEM((3 * ATTN_PAIRS, PROJ_SUB_ROWS, LANES), f32),
                        pltpu.VMEM((3 * ATTN_PAIRS, CLASSES, PROJ_SUB_ROWS // CLASSES, LANES), f32)],
        compiler_params=params(dimension_semantics=two_arb),
        name="proj_rope",
    )(x, norm_gain.reshape(1, D), w_in.astype(bf16), ret_gn_gain.reshape(1, RET_V_WIDTH), ca, sa, cr, sr)

    cm_seq = pl.BlockSpec((None, None, CLASSES, L, LANES), lambda b, hp: (b, hp, 0, 0, 0))
    r16_seq = pl.BlockSpec((None, None, MAX_DILATION, S // MAX_DILATION, LANES), lambda b, hp: (b, hp, 0, 0, 0))
    bias, ones = _attention_tables()
    attn = pl.pallas_call(
        _attn_kernel,
        grid=(B, ATTN_PAIRS),
        in_specs=[cm_seq, cm_seq, cm_seq, r16_seq, r16_seq, r16_seq, whole(bias.shape), whole(ones.shape)],
        out_specs=r16_seq,
        out_shape=jax.ShapeDtypeStruct(r16_shape, bf16),
        scratch_shapes=[pltpu.VMEM((CLASSES, L, LANES), f32)] * 3
        + [pltpu.VMEM((G, 2, BLOCK, 2 * BLOCK), bf16), pltpu.VMEM((G, BLOCK, LANES), f32)],
        compiler_params=params(dimension_semantics=two_arb),
        name="dilated_attention",
    )(qc, kc, vc, q16, k16, v16, bias, ones)

    pairs_r = RET_HEADS // 2
    C = RET_CHUNK
    assert S % (C * RET_CHUNKS_PER_STEP) == 0
    qk_spec = pl.BlockSpec((None, S, 2 * RET_KEY_DIM), lambda b, hp: (b, 0, hp))
    v_spec = pl.BlockSpec((None, S, 2 * RET_VALUE_DIM), lambda b, hp: (b, 0, hp))
    ret = pl.pallas_call(
        _ret_kernel,
        grid=(B, pairs_r),
        in_specs=[qk_spec, qk_spec, v_spec, v_spec,
                  pl.BlockSpec((1, 2, C, C), lambda b, hp: (hp, 0, 0, 0)),
                  pl.BlockSpec((1, C, 2 * RET_VALUE_DIM), lambda b, hp: (hp, 0, 0)),
                  pl.BlockSpec((1, C, 2 * RET_KEY_DIM), lambda b, hp: (hp, 0, 0)),
                  pl.BlockSpec((1, 2 * RET_KEY_DIM, 2 * RET_VALUE_DIM), lambda b, hp: (hp, 0, 0))],
        out_specs=v_spec,
        out_shape=jax.ShapeDtypeStruct((B, S, RET_V_WIDTH), bf16),
        scratch_shapes=[pltpu.VMEM((2 * RET_KEY_DIM, 2 * RET_VALUE_DIM), f32)],
        compiler_params=params(dimension_semantics=two_arb),
        name="retention",
    )(qr, kr, vr, gr, dmat, qdec, kdec, cdec)

    TO = OUT_ROW_TILE
    return pl.pallas_call(
        _out_kernel,
        grid=(S // TO, B),
        in_specs=[nat_spec(D, TO), r16_spec(TO), nat_spec(ATTN_WIDTH, TO), nat_spec(RET_V_WIDTH, TO),
                  whole(w_out.shape), whole((1, D))],
        out_specs=nat_spec(D, TO),
        out_shape=jax.ShapeDtypeStruct((B, S, D), f32),
        scratch_shapes=[pltpu.VMEM((ATTN_PAIRS, TO, LANES), f32),
                        pltpu.VMEM((ATTN_PAIRS, CLASSES, TO // CLASSES, LANES), f32)],
        compiler_params=params(dimension_semantics=two_arb),
        name="out_proj_norm",
    )(x, attn, ga, ret, w_out.astype(bf16), out_gain.reshape(1, D))


def kernel(x, norm_gain, w_in, ret_gn_gain, w_out, final_gain):
    assert norm_gain.shape[0] == 1, "the fused output projection + final norm assumes a single layer"
    assert x.shape[1] % (MAX_DILATION * BLOCK) == 0
    return _mixer_layer(x, norm_gain[0], w_in[0], ret_gn_gain[0], w_out[0], final_gain)
```

```python
import functools
import math

import numpy as np
import jax
import jax.numpy as jnp
from jax import lax
from jax.experimental import pallas as pl
from jax.experimental.pallas import tpu as pltpu

LANES = 128

ATTN_HEADS = 8
ATTN_HEAD_DIM = 64
ATTN_WIDTH = ATTN_HEADS * ATTN_HEAD_DIM
ATTN_PAIRS = ATTN_WIDTH // LANES
RET_HEADS = 4
RET_KEY_DIM = 64
RET_VALUE_DIM = 128
RET_QK_WIDTH = RET_HEADS * RET_KEY_DIM
RET_V_WIDTH = RET_HEADS * RET_VALUE_DIM
RET_CHUNK = 128
ROPE_THETA = 10000.0
NORM_EPS = 1e-6
GN_EPS = 1e-5

CLASSES = 4
BLOCK = 128
MAX_DILATION = 16
QUADS = MAX_DILATION // CLASSES
NEG_BIG = -1e30

VMEM_LIMIT_BYTES = 52 * 1024 * 1024
ROW_TILE = 1024
PROJ_SUB_ROWS = 512
OUT_ROW_TILE = 1024
BLOCKS_PER_STEP = 2


def _silu(g):
    return g / (1.0 + jnp.exp(-g))


def _retention_chunks(q_ref, k_ref, v_ref, g_ref, dmat_ref, qdec_ref, kdec_ref, cdec_ref,
                      state_ref, o_ref, batch, pair, row0, n_chunks):
    C = RET_CHUNK
    lane = lax.broadcasted_iota(jnp.int32, (C, LANES), 1)
    head0 = lane < RET_KEY_DIM
    head_masks = (head0, jnp.logical_not(head0))
    bf16 = jnp.bfloat16
    ks = slice(pair * 2 * RET_KEY_DIM, (pair + 1) * 2 * RET_KEY_DIM)
    base = pair * 2 * RET_VALUE_DIM
    qh, intra, kvs = [], [], []
    for j in range(n_chunks):
        rows = pl.ds(j * C, C)
        q = q_ref[rows, ks]
        k = k_ref[rows, ks]
        v = v_ref[rows, base:base + 2 * RET_VALUE_DIM]
        qh.append([jnp.where(head_masks[h], q, jnp.zeros_like(q)) for h in range(2)])
        o = []
        for h in range(2):
            s = lax.dot_general(qh[-1][h], k, (((1,), (1,)), ((), ())),
                                preferred_element_type=jnp.float32) * dmat_ref[pair, h]
            vs = slice(h * RET_VALUE_DIM, (h + 1) * RET_VALUE_DIM)
            o.append(jnp.dot(s.astype(bf16), v[:, vs], preferred_element_type=jnp.float32))
        intra.append(o)
        kd = (k.astype(jnp.float32) * kdec_ref[pair]).astype(bf16)
        kvs.append(lax.dot_general(kd, v, (((0,), (0,)), ((), ())), preferred_element_type=jnp.float32))
    state = state_ref[batch, pair]
    states = []
    for kv in kvs:
        states.append(state.astype(bf16))
        state = state * cdec_ref[pair] + kv
    state_ref[batch, pair] = state
    for j in range(n_chunks):
        rows = pl.ds(j * C, C)
        outs = []
        for h in range(2):
            vs = slice(h * RET_VALUE_DIM, (h + 1) * RET_VALUE_DIM)
            inter = jnp.dot(qh[j][h], states[j][:, vs], preferred_element_type=jnp.float32)
            o = intra[j][h] + inter * qdec_ref[pair, :, vs]
            mu = jnp.mean(o, axis=-1, keepdims=True)
            d = o - mu
            var = jnp.mean(d * d, axis=-1, keepdims=True)
            gate = g_ref[rows, base + h * RET_VALUE_DIM:base + (h + 1) * RET_VALUE_DIM].astype(jnp.float32)
            outs.append(d * lax.rsqrt(var + GN_EPS) * gate)
        o_ref[0, pl.ds(row0 + j * C, C), base:base + 2 * RET_VALUE_DIM] = (
            jnp.concatenate(outs, axis=1).astype(o_ref.dtype))


def _proj_kernel(x_ref, gain_ref, w_ref, gn_gain_ref, ca_ref, sa_ref, cr_ref, sr_ref,
                 dmat_ref, qdec_ref, kdec_ref, cdec_ref,
                 qc_ref, kc_ref, vc_ref, q16_ref, k16_ref, v16_ref, ga_ref, ret_ref,
                 perm_s, perm2_s, qr_s, kr_s, vr_s, gr_s, rstate_s):
    rows = PROJ_SUB_ROWS
    batch = pl.program_id(1)

    @pl.when(pl.program_id(0) == 0)
    def _():
        rstate_s[batch] = jnp.zeros(rstate_s.shape[1:], jnp.float32)

    lane = lax.broadcasted_iota(jnp.int32, (rows, LANES), 1)
    first_half = (lane % ATTN_HEAD_DIM) < (ATTN_HEAD_DIM // 2)
    even = (lane % 2) == 0
    bf16 = jnp.bfloat16

    def rope_half(slab, cos, sin):
        fwd = pltpu.roll(slab, LANES - ATTN_HEAD_DIM // 2, 1)
        bwd = pltpu.roll(slab, ATTN_HEAD_DIM // 2, 1)
        return slab * cos + jnp.where(first_half, fwd, bwd) * sin

    def rope_pairs(slab, cos, sin):
        fwd = pltpu.roll(slab, LANES - 1, 1)
        bwd = pltpu.roll(slab, 1, 1)
        return slab * cos + jnp.where(even, fwd, bwd) * sin

    def proj(h, lo, width):
        return jnp.dot(h, w_ref[:, lo:lo + width], preferred_element_type=jnp.float32)

    for sub in range(x_ref.shape[1] // rows):
        rs = pl.ds(sub * rows, rows)

        def store_permuted(cm_ref, r16_ref, hp, slot, slab):
            perm_s[slot] = slab
            for c in range(CLASSES):
                cls = perm_s[slot, pl.ds(c, rows // CLASSES, stride=CLASSES), :]
                cm_ref[0, hp, c, pl.ds(sub * rows // CLASSES, rows // CLASSES), :] = cls.astype(bf16)
                perm2_s[slot, c] = cls
                for a in range(QUADS):
                    r16_ref[0, hp, QUADS * a + c, pl.ds(sub * rows // MAX_DILATION, rows // MAX_DILATION), :] = (
                        perm2_s[slot, c, pl.ds(a, rows // MAX_DILATION, stride=QUADS), :].astype(bf16))

        x = x_ref[0, rs, :]
        ms = jnp.mean(x * x, axis=-1, keepdims=True)
        h = (x * lax.rsqrt(ms + NORM_EPS) * gain_ref[...]).astype(bf16)
        ca, sa = ca_ref[rs, :], sa_ref[rs, :]
        cr, sr = cr_ref[rs, :], sr_ref[rs, :]

        lo = 0
        pq = proj(h, lo, ATTN_WIDTH); lo += ATTN_WIDTH
        pk = proj(h, lo, ATTN_WIDTH); lo += ATTN_WIDTH
        pv = proj(h, lo, ATTN_WIDTH); lo += ATTN_WIDTH
        pg = proj(h, lo, ATTN_WIDTH); lo += ATTN_WIDTH
        q_scale = math.log2(math.e) * ATTN_HEAD_DIM ** -0.5
        for hp in range(ATTN_PAIRS):
            ls = slice(hp * LANES, (hp + 1) * LANES)
            store_permuted(qc_ref, q16_ref, hp, hp, rope_half(pq[:, ls], ca, sa) * q_scale)
            store_permuted(kc_ref, k16_ref, hp, ATTN_PAIRS + hp, rope_half(pk[:, ls], ca, sa))
            store_permuted(vc_ref, v16_ref, hp, 2 * ATTN_PAIRS + hp, pv[:, ls])
        ga_ref[0, rs, :] = _silu(pg).astype(bf16)

        pq = proj(h, lo, RET_QK_WIDTH); lo += RET_QK_WIDTH
        pk = proj(h, lo, RET_QK_WIDTH); lo += RET_QK_WIDTH
        for hp in range(RET_QK_WIDTH // LANES):
            ls = slice(hp * LANES, (hp + 1) * LANES)
            qr_s[:, ls] = rope_pairs(pq[:, ls], cr, sr).astype(bf16)
            kr_s[:, ls] = (rope_pairs(pk[:, ls], cr, sr) * (RET_KEY_DIM ** -0.5)).astype(bf16)
        pv = proj(h, lo, RET_V_WIDTH); lo += RET_V_WIDTH
        pg = proj(h, lo, RET_V_WIDTH); lo += RET_V_WIDTH
        vr_s[...] = pv.astype(bf16)
        gr_s[...] = (_silu(pg) * gn_gain_ref[...]).astype(bf16)
        for pair in range(RET_HEADS // 2):
            _retention_chunks(qr_s, kr_s, vr_s, gr_s, dmat_ref, qdec_ref, kdec_ref, cdec_ref,
                              rstate_s, ret_ref, batch, pair, sub * rows, rows // RET_CHUNK)


def _attn_kernel(qc_ref, kc_ref, vc_ref, q16_ref, k16_ref, v16_ref, bias_ref, ones_ref, o_ref,
                 acc_s, m_s, den_s):
    G = BLOCKS_PER_STEP
    L = qc_ref.shape[1]
    lane = lax.broadcasted_iota(jnp.int32, (BLOCK, LANES), 1)
    head0 = lane < ATTN_HEAD_DIM
    bf16 = jnp.bfloat16
    state = (acc_s, m_s, den_s)
    sub = BLOCK // CLASSES
    blocks3 = L // QUADS // BLOCK

    def twin(ref):
        return q16_ref if ref is qc_ref else k16_ref if ref is kc_ref else v16_ref

    def segments(branch, t):
        if branch == 0:
            def load(ref, n):
                return jnp.concatenate([ref[c, pl.ds(n * sub, sub), :] for c in range(CLASSES)], axis=0)

            def store(ref, n, val):
                for c in range(CLASSES):
                    ref[c, pl.ds(n * sub, sub), :] = val[c * sub:(c + 1) * sub, :]

            return [(load, load, store, [t * G + g for g in range(G)])]
        if branch == 1:
            per_class = L // BLOCK // G
            c = t // per_class

            def load(ref, n):
                return ref[c, pl.ds(n * BLOCK, BLOCK), :]

            def store(ref, n, val):
                ref[c, pl.ds(n * BLOCK, BLOCK), :] = val

            return [(load, load, store, [(t % per_class) * G + g for g in range(G)])]
        segs = []
        flat = [t * G + g for g in range(G)]
        for r in sorted({b // blocks3 for b in flat}):
            c = r % CLASSES
            a = r // CLASSES

            def load_in(ref, n, r=r):
                return twin(ref)[r, pl.ds(n * BLOCK, BLOCK), :]

            def load_state(ref, n, c=c, a=a):
                return ref[c, pl.ds(n * BLOCK * QUADS + a, BLOCK, stride=QUADS), :]

            def store(ref, n, val, r=r, c=c, a=a):
                if ref is o_ref:
                    o_ref[r, pl.ds(n * BLOCK, BLOCK), :] = val.astype(o_ref.dtype)
                else:
                    ref[c, pl.ds(n * BLOCK * QUADS + a, BLOCK, stride=QUADS), :] = val

            segs.append((load_in, load_state, store, [b % blocks3 for b in flat if b // blocks3 == r]))
        return segs

    n_groups = (CLASSES * L // BLOCK // G, CLASSES * L // BLOCK // G, MAX_DILATION * blocks3 // G)

    def key_blocks(blocks):
        return blocks if blocks[0] == 0 else [blocks[0] - 1] + blocks

    def stage_scores(branch, t):
        out = []
        for load_in, _, _, blocks in segments(branch, t):
            kb = {nb: load_in(kc_ref, nb) for nb in key_blocks(blocks)}
            for n in blocks:
                if n == 0:
                    keys, bias = kb[n], bias_ref[branch, :, pl.ds(BLOCK, BLOCK)]
                else:
                    keys, bias = jnp.concatenate([kb[n - 1], kb[n]], axis=0), bias_ref[branch]
                q = load_in(qc_ref, n)
                heads = []
                for h in range(2):
                    qh = q * ones_ref[h]
                    s = lax.dot_general(qh, keys, (((1,), (1,)), ((), ())),
                                        preferred_element_type=jnp.float32) + bias
                    heads.append((s, jnp.max(s, axis=1, keepdims=True)))
                out.append(heads)
        return out

    def stage_softmax(scores):
        out = []
        for heads in scores:
            probs = jnp.concatenate([jnp.exp2(s - m).astype(bf16) for s, m in heads], axis=1)
            out.append((probs, jnp.where(head0, heads[0][1], heads[1][1])))
        return out

    def stage_values(branch, t, probs_in):
        i = 0
        out = []
        for load_in, load_state, store, blocks in segments(branch, t):
            wb = {}
            for nb in key_blocks(blocks):
                v = load_in(vc_ref, nb)
                wb[nb] = [jnp.concatenate([v * ones_ref[h], ones_ref[h]], axis=1) for h in range(2)]
            for n in blocks:
                if n == 0:
                    weights = jnp.concatenate([wb[n][0], wb[n][1]], axis=0)
                else:
                    weights = jnp.concatenate([wb[n - 1][0], wb[n][0], wb[n - 1][1], wb[n][1]], axis=0)
                probs, m = probs_in[i]
                r = jnp.dot(probs, weights, preferred_element_type=jnp.float32)
                out.append((load_state, store, n, r[:, :LANES], m, r[:, LANES:]))
                i += 1
        return out

    def stage_merge(branch, values):
        for load_state, store, n, acc, m, den in values:
            if branch > 0:
                acc_o, m_o, den_o = (load_state(ref, n) for ref in state)
                m_n, m = m, jnp.maximum(m_o, m)
                w_o = jnp.exp2(m_o - m)
                w_n = jnp.exp2(m_n - m)
                acc = w_o * acc_o + w_n * acc
                den = w_o * den_o + w_n * den
            if branch == 2:
                store(o_ref, n, acc / den)
            else:
                for ref, val in zip(state, (acc, m, den)):
                    store(ref, n, val)

    per_branch = n_groups[0]
    assert n_groups == (per_branch,) * 3
    total = 3 * per_branch
    scores, probs, values = {}, {}, {}
    for tau in range(total + 3):
        if tau < total:
            scores[tau] = stage_scores(tau // per_branch, tau % per_branch)
        if 0 <= tau - 1 < total:
            probs[tau - 1] = stage_softmax(scores.pop(tau - 1))
        if 0 <= tau - 2 < total:
            values[tau - 2] = stage_values((tau - 2) // per_branch, (tau - 2) % per_branch, probs.pop(tau - 2))
        if 0 <= tau - 3 < total:
            stage_merge((tau - 3) // per_branch, values.pop(tau - 3))


def _out_kernel(x_ref, a_ref, ga_ref, r_ref, w_ref, gain_ref, o_ref, perm_s, perm2_s):
    rows = x_ref.shape[1]
    for hp in range(ATTN_PAIRS):
        for c in range(CLASSES):
            for a in range(QUADS):
                perm2_s[hp, c, pl.ds(a, rows // MAX_DILATION, stride=QUADS), :] = (
                    a_ref[0, hp, QUADS * a + c].astype(jnp.float32))
            perm_s[hp, pl.ds(c, rows // CLASSES, stride=CLASSES), :] = perm2_s[hp, c]
    attn = jnp.concatenate([perm_s[hp] for hp in range(ATTN_PAIRS)], axis=1)
    gated = (attn * ga_ref[0].astype(jnp.float32)).astype(jnp.bfloat16)
    mixed = jnp.concatenate([gated, r_ref[0]], axis=1)
    y = x_ref[0] + jnp.dot(mixed, w_ref[...], preferred_element_type=jnp.float32)
    ms = jnp.mean(y * y, axis=-1, keepdims=True)
    o_ref[0] = y * lax.rsqrt(ms + NORM_EPS) * gain_ref[...]


def _rotation_tables(seq):
    pos = jnp.arange(seq).astype(jnp.float32)
    half = ATTN_HEAD_DIM // 2
    inv = ROPE_THETA ** (-jnp.arange(half, dtype=jnp.float32) / half)
    ang = pos[:, None] * inv[None, :]
    cos, sin = jnp.cos(ang), jnp.sin(ang)
    ca = jnp.tile(jnp.concatenate([cos, cos], axis=1), (1, LANES // ATTN_HEAD_DIM))
    sa = jnp.tile(jnp.concatenate([-sin, sin], axis=1), (1, LANES // ATTN_HEAD_DIM))
    half = RET_KEY_DIM // 2
    inv = 1.0 / (ROPE_THETA ** jnp.linspace(0.0, 1.0, half, dtype=jnp.float32))
    ang = pos[:, None] * inv[None, :]
    cos, sin = jnp.cos(ang), jnp.sin(ang)
    cr = jnp.tile(jnp.repeat(cos, 2, axis=1), (1, LANES // RET_KEY_DIM))
    sr = jnp.tile(jnp.stack([-sin, sin], axis=-1).reshape(seq, RET_KEY_DIM), (1, LANES // RET_KEY_DIM))
    return ca, sa, cr, sr


def _attention_tables():
    i = np.arange(BLOCK)
    sz = BLOCK // CLASSES
    orders = [
        CLASSES * (i % sz) + i // sz,
        i,
        i,
    ]
    bias = np.zeros((3, BLOCK, 2 * BLOCK), np.float32)
    for b, u in enumerate(orders):
        prev_ok = u[None, :] >= u[:, None]
        cur_ok = u[None, :] <= u[:, None]
        bias[b] = np.where(np.concatenate([prev_ok, cur_ok], axis=1), 0.0, NEG_BIG)
    lane_head = np.arange(LANES) // ATTN_HEAD_DIM
    ones = np.broadcast_to((lane_head[None, :] == np.arange(2)[:, None])[:, None, :], (2, BLOCK, LANES))
    return jnp.asarray(bias), jnp.asarray(ones, dtype=jnp.bfloat16)


def _retention_tables():
    C = RET_CHUNK
    log_gamma = jnp.log1p(-(2.0 ** (-5.0 - jnp.arange(RET_HEADS, dtype=jnp.float32))))
    cpos = jnp.arange(C, dtype=jnp.float32)
    diff = cpos[:, None] - cpos[None, :]
    dmat = jnp.where(diff[None] >= 0, jnp.exp(jnp.maximum(diff, 0.0)[None] * log_gamma[:, None, None]), 0.0)
    k_dec = jnp.exp((C - 1 - cpos)[None, :] * log_gamma[:, None])
    q_dec = jnp.exp((cpos + 1)[None, :] * log_gamma[:, None])
    chunk_decay = jnp.exp(C * log_gamma)
    pairs = RET_HEADS // 2
    dmat = dmat.reshape(pairs, 2, C, C)
    qdec = jnp.repeat(q_dec.reshape(pairs, 2, C).transpose(0, 2, 1), RET_VALUE_DIM, axis=2)
    kdec = jnp.repeat(k_dec.reshape(pairs, 2, C).transpose(0, 2, 1), RET_KEY_DIM, axis=2)
    row_head = jnp.arange(2 * RET_KEY_DIM) // RET_KEY_DIM
    cdec = jnp.broadcast_to(chunk_decay.reshape(pairs, 2)[:, row_head][:, :, None],
                            (pairs, 2 * RET_KEY_DIM, 2 * RET_VALUE_DIM))
    return dmat, qdec, kdec, cdec


def _mixer_layer(x, norm_gain, w_in, ret_gn_gain, w_out, out_gain):
    B, S, D = x.shape
    L = S // CLASSES
    TR = ROW_TILE
    G = BLOCKS_PER_STEP
    f32, bf16 = jnp.float32, jnp.bfloat16
    params = functools.partial(pltpu.CompilerParams, vmem_limit_bytes=VMEM_LIMIT_BYTES)
    two_arb = ("arbitrary", "arbitrary")

    with jax.ensure_compile_time_eval(), jax.default_device(jax.devices("cpu")[0]):
        ca, sa, cr, sr = _rotation_tables(S)
        dmat, qdec, kdec, cdec = _retention_tables()
    tab_spec = pl.BlockSpec((TR, LANES), lambda m, b: (m, 0))
    cm_shape = (B, ATTN_PAIRS, CLASSES, L, LANES)
    cm_spec = pl.BlockSpec((1, ATTN_PAIRS, CLASSES, TR // CLASSES, LANES), lambda m, b: (b, 0, 0, m, 0))
    r16_shape = (B, ATTN_PAIRS, MAX_DILATION, S // MAX_DILATION, LANES)

    def r16_spec(tr):
        return pl.BlockSpec((1, ATTN_PAIRS, MAX_DILATION, tr // MAX_DILATION, LANES), lambda m, b: (b, 0, 0, m, 0))

    def nat_spec(width, tr=TR):
        return pl.BlockSpec((1, tr, width), lambda m, b: (b, m, 0))

    def whole(shape):
        return pl.BlockSpec(shape, lambda *_: (0,) * len(shape))

    pairs_r = RET_HEADS // 2
    qc, kc, vc, q16, k16, v16, ga, ret = pl.pallas_call(
        _proj_kernel,
        grid=(S // TR, B),
        in_specs=[nat_spec(D), whole((1, D)),
                  pl.BlockSpec(w_in.shape, lambda m, b: (0, 0), pipeline_mode=pl.Buffered(1)),
                  whole((1, RET_V_WIDTH)),
                  tab_spec, tab_spec, tab_spec, tab_spec,
                  whole(dmat.shape), whole(qdec.shape), whole(kdec.shape), whole(cdec.shape)],
        out_specs=[cm_spec, cm_spec, cm_spec, r16_spec(TR), r16_spec(TR), r16_spec(TR), nat_spec(ATTN_WIDTH),
                   nat_spec(RET_V_WIDTH)],
        out_shape=[jax.ShapeDtypeStruct(cm_shape, bf16)] * 3 + [jax.ShapeDtypeStruct(r16_shape, bf16)] * 3 + [
            jax.ShapeDtypeStruct((B, S, ATTN_WIDTH), bf16), jax.ShapeDtypeStruct((B, S, RET_V_WIDTH), bf16)],
        scratch_shapes=[pltpu.VMEM((3 * ATTN_PAIRS, PROJ_SUB_ROWS, LANES), f32),
                        pltpu.VMEM((3 * ATTN_PAIRS, CLASSES, PROJ_SUB_ROWS // CLASSES, LANES), f32),
                        pltpu.VMEM((PROJ_SUB_ROWS, RET_QK_WIDTH), bf16), pltpu.VMEM((PROJ_SUB_ROWS, RET_QK_WIDTH), bf16),
                        pltpu.VMEM((PROJ_SUB_ROWS, RET_V_WIDTH), bf16), pltpu.VMEM((PROJ_SUB_ROWS, RET_V_WIDTH), bf16),
                        pltpu.VMEM((B, pairs_r, 2 * RET_KEY_DIM, 2 * RET_VALUE_DIM), f32)],
        compiler_params=params(dimension_semantics=two_arb),
        name="proj_rope",
    )(x, norm_gain.reshape(1, D), w_in.astype(bf16), ret_gn_gain.reshape(1, RET_V_WIDTH), ca, sa, cr, sr,
      dmat, qdec, kdec, cdec)

    cm_seq = pl.BlockSpec((None, None, CLASSES, L, LANES), lambda b, hp: (b, hp, 0, 0, 0))
    r16_seq = pl.BlockSpec((None, None, MAX_DILATION, S // MAX_DILATION, LANES), lambda b, hp: (b, hp, 0, 0, 0))
    bias, ones = _attention_tables()
    attn = pl.pallas_call(
        _attn_kernel,
        grid=(B, ATTN_PAIRS),
        in_specs=[cm_seq, cm_seq, cm_seq, r16_seq, r16_seq, r16_seq, whole(bias.shape), whole(ones.shape)],
        out_specs=r16_seq,
        out_shape=jax.ShapeDtypeStruct(r16_shape, bf16),
        scratch_shapes=[pltpu.VMEM((CLASSES, L, LANES), f32)] * 3,
        compiler_params=params(dimension_semantics=two_arb),
        name="dilated_attention",
    )(qc, kc, vc, q16, k16, v16, bias, ones)

    TO = OUT_ROW_TILE
    return pl.pallas_call(
        _out_kernel,
        grid=(S // TO, B),
        in_specs=[nat_spec(D, TO), r16_spec(TO), nat_spec(ATTN_WIDTH, TO), nat_spec(RET_V_WIDTH, TO),
                  whole(w_out.shape), whole((1, D))],
        out_specs=nat_spec(D, TO),
        out_shape=jax.ShapeDtypeStruct((B, S, D), f32),
        scratch_shapes=[pltpu.VMEM((ATTN_PAIRS, TO, LANES), f32),
                        pltpu.VMEM((ATTN_PAIRS, CLASSES, TO // CLASSES, LANES), f32)],
        compiler_params=params(dimension_semantics=two_arb),
        name="out_proj_norm",
    )(x, attn, ga, ret, w_out.astype(bf16), out_gain.reshape(1, D))


def kernel(x, norm_gain, w_in, ret_gn_gain, w_out, final_gain):
    assert norm_gain.shape[0] == 1, "the fused output projection + final norm assumes a single layer"
    assert x.shape[1] % (MAX_DILATION * BLOCK) == 0
    return _mixer_layer(x, norm_gain[0], w_in[0], ret_gn_gain[0], w_out[0], final_gain)
```
